```python
import math
import jax, jax.numpy as jnp
from jax import lax
import numpy as np

D_MODEL = 1024
BATCH = 8
SEQ = 2048
DEPTH = 2
DEC_BATCH = 8
DEC_SEQ = 32
PAST_LEN = 2048

CHUNK = 64
N_MIXERS = 2
N_CONV_LAYERS = (DEPTH + 1) // 2
N_ATTN_LAYERS = DEPTH // 2
CONV_W = 31
HEAD_DIM = 64
N_HEADS = D_MODEL // HEAD_DIM
N_KV = 4
GROUP = N_HEADS // N_KV
WINDOW = 128
NBACK = WINDOW // CHUNK
D_FF = ((8 * D_MODEL // 3 + 255) // 256) * 256
SCALE = 1.0 / math.sqrt(HEAD_DIM)
NEG = -1e30
RMS_EPS = 1e-6
LN_EPS = 1e-5

kernel_name = "chunk_causal_conformer_swa_sink_hybrid_step"


def rms_norm(x, g):
    xf = x.astype(jnp.float32)
    y = xf * lax.rsqrt(jnp.mean(xf * xf, axis=-1, keepdims=True) + RMS_EPS)
    return (y * g.astype(jnp.float32)).astype(x.dtype)


def layer_norm(x, g, b):
    xf = x.astype(jnp.float32)
    mu = jnp.mean(xf, axis=-1, keepdims=True)
    xc = xf - mu
    y = xc * lax.rsqrt(jnp.mean(xc * xc, axis=-1, keepdims=True) + LN_EPS)
    return (y * g.astype(jnp.float32) + b.astype(jnp.float32)).astype(x.dtype)


def alibi_slopes():
    h = jnp.arange(1, N_HEADS + 1, dtype=jnp.float32)
    return jnp.exp2(-8.0 * h / N_HEADS).reshape(N_KV, GROUP)


def swiglu_ffn(h, w_gate, w_up, w_down):
    return (jax.nn.silu(h @ w_gate) * (h @ w_up)) @ w_down


def conv_glu_in(h, w_pw1, b_pw1):
    u = h @ w_pw1 + b_pw1
    a, gate = jnp.split(u, 2, axis=-1)
    return a * jax.nn.sigmoid(gate)


def conv_tail(g_ext, w_dw, b_dw, ln_g, ln_b, w_pw2, b_pw2):
    c = lax.conv_general_dilated(
        g_ext, w_dw[:, None, :].astype(g_ext.dtype), window_strides=(1,), padding='VALID',
        dimension_numbers=('NWC', 'WIO', 'NWC'), feature_group_count=D_MODEL) + b_dw
    c = jax.nn.silu(layer_norm(c, ln_g, ln_b))
    return c @ w_pw2 + b_pw2


def qkv_proj(h, w_qkv, qn_g, kn_g):
    B, T, _ = h.shape
    u = h @ w_qkv
    q, k, v = jnp.split(u, [N_HEADS * HEAD_DIM, (N_HEADS + N_KV) * HEAD_DIM], axis=-1)
    q = rms_norm(q.reshape(B, T, N_HEADS, HEAD_DIM), qn_g)
    k = rms_norm(k.reshape(B, T, N_KV, HEAD_DIM), kn_g)
    v = v.reshape(B, T, N_KV, HEAD_DIM)
    return q, k, v


def band_dist_mask(qpos, kpos):
    dist = jnp.abs(qpos[:, :, None] - kpos[:, None, :]).astype(jnp.float32)
    qc = (qpos // CHUNK)[:, :, None]
    kc = (kpos // CHUNK)[:, None, :]
    valid = (kpos >= 0)[:, None, :] & (kc <= qc) & (kc >= qc - NBACK)
    return dist, valid


def sink_attention(q, k, v, dist, valid, sinks):
    s = jnp.einsum('bnqkgd,bnskd->bkgnqs', q, k).astype(jnp.float32) * SCALE
    s = s - alibi_slopes()[:, :, None, None, None] * dist
    s = jnp.where(valid, s, NEG)
    sk = sinks.astype(jnp.float32).reshape(N_KV, GROUP)[:, :, None, None, None]
    m = jnp.maximum(jnp.max(s, axis=-1, keepdims=True), sk)
    p = jnp.exp(s - m)
    denom = jnp.sum(p, axis=-1, keepdims=True) + jnp.exp(sk - m)
    p = (p / denom).astype(v.dtype)
    return jnp.einsum('bkgnqs,bnskd->bnqkgd', p, v)


def swa_prompt(q, k, v, sinks, w_o):
    B, T = q.shape[0], q.shape[1]
    n_c = T // CHUNK
    qb = q.reshape(B, n_c, CHUNK, N_KV, GROUP, HEAD_DIM)
    pad = ((0, 0), (NBACK * CHUNK, 0), (0, 0), (0, 0))
    kp, vp = jnp.pad(k, pad), jnp.pad(v, pad)
    kb = jnp.concatenate([kp[:, j * CHUNK: j * CHUNK + T].reshape(B, n_c, CHUNK, N_KV, HEAD_DIM)
                          for j in range(NBACK + 1)], axis=2)
    vb = jnp.concatenate([vp[:, j * CHUNK: j * CHUNK + T].reshape(B, n_c, CHUNK, N_KV, HEAD_DIM)
                          for j in range(NBACK + 1)], axis=2)
    c_idx = jnp.arange(n_c, dtype=jnp.int32)[:, None]
    qpos = c_idx * CHUNK + jnp.arange(CHUNK, dtype=jnp.int32)[None, :]
    kpos = (c_idx - NBACK) * CHUNK + jnp.arange((NBACK + 1) * CHUNK, dtype=jnp.int32)[None, :]
    dist, valid = band_dist_mask(qpos, kpos)
    o = sink_attention(qb, kb, vb, dist, valid, sinks)
    return o.reshape(B, T, N_HEADS * HEAD_DIM) @ w_o


def swa_sample(q, k_all, v_all, sinks, w_o):
    B, T = q.shape[0], q.shape[1]
    qb = q.reshape(B, 1, T, N_KV, GROUP, HEAD_DIM)
    qpos = (PAST_LEN + jnp.arange(T, dtype=jnp.int32))[None, :]
    kpos = jnp.concatenate([PAST_LEN - WINDOW + jnp.arange(WINDOW, dtype=jnp.int32),
                            PAST_LEN + jnp.arange(T, dtype=jnp.int32)])[None, :]
    dist, valid = band_dist_mask(qpos, kpos)
    o = sink_attention(qb, k_all[:, None], v_all[:, None], dist, valid, sinks)
    return o.reshape(B, T, N_HEADS * HEAD_DIM) @ w_o


def setup_inputs(seed: int = 0) -> dict:
    key = jax.random.key(seed)
    ks = jax.random.split(key, 32)
    f32 = jnp.float32
    nrm = lambda k, shape, s: jax.random.normal(k, shape, f32) * s
    Lc, La, D = N_CONV_LAYERS, N_ATTN_LAYERS, D_MODEL
    qkv_w = (N_HEADS + 2 * N_KV) * HEAD_DIM
    return {
        "x_prompt": nrm(ks[0], (BATCH, SEQ, D), 1.0),
        "x_sample": nrm(ks[1], (DEC_BATCH, DEC_SEQ, D), 1.0),
        "state_conv": nrm(ks[2], (Lc, DEC_BATCH, CONV_W - 1, D), 0.5),
        "cache_k": nrm(ks[3], (La, DEC_BATCH, WINDOW, N_KV, HEAD_DIM), 1.0),
        "cache_v": nrm(ks[4], (La, DEC_BATCH, WINDOW, N_KV, HEAD_DIM), 1.0),
        "g_mix": 1.0 + nrm(ks[5], (DEPTH, D), 0.05),
        "g_ffn": 1.0 + nrm(ks[6], (DEPTH, D), 0.05),
        "w_pw1": nrm(ks[7], (Lc, D, 2 * D), D ** -0.5),
        "b_pw1": nrm(ks[8], (Lc, 2 * D), 0.01),
        "w_dw": nrm(ks[9], (Lc, CONV_W, D), CONV_W ** -0.5),
        "b_dw": nrm(ks[10], (Lc, D), 0.01),
        "ln_g": 1.0 + nrm(ks[11], (Lc, D), 0.05),
        "ln_b": nrm(ks[12], (Lc, D), 0.01),
        "w_pw2": nrm(ks[13], (Lc, D, D), D ** -0.5),
        "b_pw2": nrm(ks[14], (Lc, D), 0.01),
        "w_qkv": nrm(ks[15], (La, D, qkv_w), D ** -0.5),
        "qn_g": 1.0 + nrm(ks[16], (La, HEAD_DIM), 0.05),
        "kn_g": 1.0 + nrm(ks[17], (La, HEAD_DIM), 0.05),
        "sinks": nrm(ks[18], (La, N_HEADS), 1.0),
        "w_o": nrm(ks[19], (La, N_HEADS * HEAD_DIM, D), (N_HEADS * HEAD_DIM) ** -0.5),
        "w_gate": nrm(ks[20], (DEPTH, D, D_FF), D ** -0.5),
        "w_up": nrm(ks[21], (DEPTH, D, D_FF), D ** -0.5),
        "w_down": nrm(ks[22], (DEPTH, D_FF, D), D_FF ** -0.5),
    }


def reference(x_prompt, x_sample, state_conv, cache_k, cache_v,
              g_mix, g_ffn, w_pw1, b_pw1, w_dw, b_dw, ln_g, ln_b, w_pw2, b_pw2,
              w_qkv, qn_g, kn_g, sinks, w_o, w_gate, w_up, w_down):
    xp, xs = x_prompt, x_sample
    conv_p, conv_s, kp_l, vp_l, ks_l, vs_l = [], [], [], [], [], []
    for i in range(DEPTH):
        hp, hs = rms_norm(xp, g_mix[i]), rms_norm(xs, g_mix[i])
        j = i // N_MIXERS
        if i % N_MIXERS == 0:
            gp = conv_glu_in(hp, w_pw1[j], b_pw1[j])
            gs = conv_glu_in(hs, w_pw1[j], b_pw1[j])
            gp_ext = jnp.pad(gp, ((0, 0), (CONV_W - 1, 0), (0, 0)))
            gs_ext = jnp.concatenate([state_conv[j].astype(gs.dtype), gs], axis=1)
            mp = conv_tail(gp_ext, w_dw[j], b_dw[j], ln_g[j], ln_b[j], w_pw2[j], b_pw2[j])
            ms = conv_tail(gs_ext, w_dw[j], b_dw[j], ln_g[j], ln_b[j], w_pw2[j], b_pw2[j])
            conv_p.append(gp_ext[:, -(CONV_W - 1):])
            conv_s.append(gs_ext[:, -(CONV_W - 1):])
        else:
            qp, kp, vp = qkv_proj(hp, w_qkv[j], qn_g[j], kn_g[j])
            qs, kn, vn = qkv_proj(hs, w_qkv[j], qn_g[j], kn_g[j])
            mp = swa_prompt(qp, kp, vp, sinks[j], w_o[j])
            k_all = jnp.concatenate([cache_k[j].astype(kn.dtype), kn], axis=1)
            v_all = jnp.concatenate([cache_v[j].astype(vn.dtype), vn], axis=1)
            ms = swa_sample(qs, k_all, v_all, sinks[j], w_o[j])
            kp_l.append(kp[:, -WINDOW:])
            vp_l.append(vp[:, -WINDOW:])
            ks_l.append(k_all[:, -WINDOW:])
            vs_l.append(v_all[:, -WINDOW:])
        xp = xp + mp
        xs = xs + ms
        xp = xp + swiglu_ffn(rms_norm(xp, g_ffn[i]), w_gate[i], w_up[i], w_down[i])
        xs = xs + swiglu_ffn(rms_norm(xs, g_ffn[i]), w_gate[i], w_up[i], w_down[i])
    new_conv_prompt = jnp.stack(conv_p)
    new_conv_sample = jnp.stack(conv_s)
    new_k_prompt = jnp.stack(kp_l)
    new_v_prompt = jnp.stack(vp_l)
    new_k_sample = jnp.stack(ks_l)
    new_v_sample = jnp.stack(vs_l)
    return (xp, xs, new_conv_prompt, new_conv_sample, new_k_prompt, new_v_prompt, new_k_sample, new_v_sample)
```

```python
import functools

import jax
import jax.numpy as jnp
from jax import lax
from jax.experimental import pallas as pl
from jax.experimental.pallas import tpu as pltpu

D_MODEL = 1024
CHUNK = 64
CONV_W = 31
HEAD_DIM = 64
N_HEADS = D_MODEL // HEAD_DIM
N_KV = 4
GROUP = N_HEADS // N_KV
WINDOW = 128
D_FF = 2816
QKV_W = (N_HEADS + 2 * N_KV) * HEAD_DIM
KV_W = N_KV * HEAD_DIM
SCALE = HEAD_DIM ** -0.5
NEG = -1e30
RMS_EPS = 1e-6
LN_EPS = 1e-5
PAST_LEN = 2048

SUBLANES = 8
HALO = -(-(CONV_W - 1) // SUBLANES) * SUBLANES
HALO_PAD = HALO - (CONV_W - 1)
MXU_N = 256
CONV_ROWS = 32
CONV_COLS = 512
VMEM_LIMIT = 56 * 1024 * 1024

F32 = jnp.float32
BF16 = jnp.bfloat16


def _dot(a, b):
    return jnp.dot(a, b, preferred_element_type=F32)


def _rms(x, g):
    ms = jnp.mean(x * x, axis=-1, keepdims=True)
    return x * lax.rsqrt(ms + RMS_EPS) * g


def _ffn(x1, gf_ref, wg_ref, wu_ref, wd_ref, act_ref):
    hn = _rms(x1, gf_ref[...]).astype(BF16)
    for j in range(D_FF // MXU_N):
        sl = slice(j * MXU_N, (j + 1) * MXU_N)
        gt = _dot(hn, wg_ref[:, sl])
        up = _dot(hn, wu_ref[:, sl])
        act_ref[:, sl] = (gt * jax.nn.sigmoid(gt) * up).astype(BF16)
    return x1 + _dot(act_ref[...], wd_ref[...])


def _layer0_kernel(x_ref, st_ref, gm_ref, wpw1_ref, bpw1_ref, wdw_ref, bdw_ref, lng_ref, lnb_ref,
                   wpw2_ref, bpw2_ref, gf_ref, wg_ref, wu_ref, wd_ref,
                   y_ref, nst_ref, gext_ref, cbuf_ref, act_ref, *, ns, tm, multi_step):
    m = ns * tm
    t = pl.program_id(1)

    @pl.when(t == 0)
    def _():
        gext_ref[:, :HALO, :] = st_ref[...]

    x = x_ref[...].reshape(m, D_MODEL)
    h = _rms(x, gm_ref[...]).astype(BF16)
    a = _dot(h, wpw1_ref[:, :D_MODEL]) + bpw1_ref[:, :D_MODEL]
    gate = _dot(h, wpw1_ref[:, D_MODEL:]) + bpw1_ref[:, D_MODEL:]
    g = a * jax.nn.sigmoid(gate)
    gext_ref[:, HALO:, :] = g.reshape(ns, tm, D_MODEL)

    for s in range(ns):
        for r0 in range(0, tm, CONV_ROWS):
            for c0 in range(0, D_MODEL, CONV_COLS):
                cs = slice(c0, c0 + CONV_COLS)
                acc = None
                for k in range(CONV_W):
                    term = gext_ref[s, pl.ds(HALO_PAD + k + r0, CONV_ROWS), cs] * wdw_ref[k:k + 1, cs]
                    acc = term if acc is None else acc + term
                cbuf_ref[pl.ds(s * tm + r0, CONV_ROWS), cs] = acc + bdw_ref[:, cs]

    nst_ref[...] = gext_ref[:, tm:, :]
    if multi_step:
        gext_ref[:, :HALO, :] = gext_ref[:, tm:, :]

    c = cbuf_ref[...]
    mu = jnp.mean(c, axis=-1, keepdims=True)
    xc = c - mu
    ln = xc * lax.rsqrt(jnp.mean(xc * xc, axis=-1, keepdims=True) + LN_EPS) * lng_ref[...] + lnb_ref[...]
    sw = (ln * jax.nn.sigmoid(ln)).astype(BF16)
    x1 = x + _dot(sw, wpw2_ref[...]) + bpw2_ref[...]
    y_ref[...] = _ffn(x1, gf_ref, wg_ref, wu_ref, wd_ref, act_ref).reshape(ns, tm, D_MODEL)


def _head_mean_square(z):
    r = lax.broadcasted_iota(jnp.int32, (MXU_N, MXU_N), 0) // HEAD_DIM
    c = lax.broadcasted_iota(jnp.int32, (MXU_N, MXU_N), 1) // HEAD_DIM
    ones_bd = jnp.where(r == c, 1.0, 0.0).astype(BF16)
    outs = []
    for j in range(z.shape[1] // MXU_N):
        zz = z[:, j * MXU_N:(j + 1) * MXU_N]
        zz = zz * zz
        hi = zz.astype(BF16)
        lo = (zz - hi.astype(F32)).astype(BF16)
        outs.append(_dot(hi, ones_bd) + _dot(lo, ones_bd))
    ss = outs[0] if len(outs) == 1 else jnp.concatenate(outs, axis=1)
    return ss * (1.0 / HEAD_DIM)


def _layer1_kernel(*refs, ns, tm, tq, prompt, multi_step):
    if prompt:
        (x_ref, sinks_ref, gm_ref, wqkv_ref, qg_ref, kg_ref, wo_ref, gf_ref, wg_ref, wu_ref, wd_ref,
         y_ref, ko_ref, vo_ref, qn_ref, kbuf_ref, vbuf_ref, ao_ref, act_ref, bias_ref) = refs
        kc_ref = vc_ref = None
    else:
        (x_ref, kc_ref, vc_ref, sinks_ref, gm_ref, wqkv_ref, qg_ref, kg_ref, wo_ref, gf_ref, wg_ref,
         wu_ref, wd_ref, y_ref, ko_ref, vo_ref, qn_ref, kbuf_ref, vbuf_ref, ao_ref, act_ref,
         bias_ref) = refs
    m = ns * tm
    nk = WINDOW + tq
    rows4 = GROUP * tq
    b = pl.program_id(0)
    t = pl.program_id(1)

    @pl.when((b == 0) & (t == 0))
    def _():
        rows = lax.broadcasted_iota(jnp.int32, (rows4, nk), 0)
        cols = lax.broadcasted_iota(jnp.int32, (rows4, nk), 1)
        dist = jnp.abs((rows % tq) + WINDOW - cols).astype(F32)
        grp = rows // tq
        for kvh in range(N_KV):
            slope = jnp.zeros((rows4, nk), F32)
            for g_ in range(GROUP):
                hh = kvh * GROUP + g_
                slope = jnp.where(grp == g_, 2.0 ** (-8.0 * (hh + 1) / N_HEADS), slope)
            bias_ref[kvh] = -(slope * dist)

    @pl.when(t == 0)
    def _():
        if prompt:
            kbuf_ref[:, :WINDOW, :] = jnp.zeros((ns, WINDOW, KV_W), BF16)
            vbuf_ref[:, :WINDOW, :] = jnp.zeros((ns, WINDOW, KV_W), BF16)
        else:
            kbuf_ref[:, :WINDOW, :] = kc_ref[...].astype(BF16)
            vbuf_ref[:, :WINDOW, :] = vc_ref[...].astype(BF16)

    x = x_ref[...].reshape(m, D_MODEL)
    h = _rms(x, gm_ref[...]).astype(BF16)
    q = _dot(h, wqkv_ref[:, :D_MODEL])
    k = _dot(h, wqkv_ref[:, D_MODEL:D_MODEL + KV_W])
    v = _dot(h, wqkv_ref[:, D_MODEL + KV_W:])
    qn = q * lax.rsqrt(_head_mean_square(q) + RMS_EPS) * (qg_ref[...] * SCALE)
    kn = k * lax.rsqrt(_head_mean_square(k) + RMS_EPS) * kg_ref[...]
    qn_ref[...] = qn.astype(BF16)
    kbuf_ref[:, WINDOW:, :] = kn.astype(BF16).reshape(ns, tm, KV_W)
    vbuf_ref[:, WINDOW:, :] = v.astype(BF16).reshape(ns, tm, KV_W)
    kn3 = kn.reshape(ns, tm, KV_W)
    v3 = v.reshape(ns, tm, KV_W)
    if tm >= WINDOW:
        ko_ref[...] = kn3[:, tm - WINDOW:, :]
        vo_ref[...] = v3[:, tm - WINDOW:, :]
    else:
        ko_ref[:, :WINDOW - tm, :] = kc_ref[:, tm:, :]
        vo_ref[:, :WINDOW - tm, :] = vc_ref[:, tm:, :]
        ko_ref[:, WINDOW - tm:, :] = kn3
        vo_ref[:, WINDOW - tm:, :] = v3

    grp_col = lax.broadcasted_iota(jnp.int32, (rows4, 1), 0) // tq
    sink_cols = []
    for kvh in range(N_KV):
        sc = jnp.zeros((rows4, 1), F32)
        for g_ in range(GROUP):
            sc = jnp.where(grp_col == g_, sinks_ref[kvh * GROUP + g_], sc)
        sink_cols.append(sc)
    key_col = lax.broadcasted_iota(jnp.int32, (rows4, nk), 1)
    n_chunks = tm // tq

    for s in range(ns):
        def chunk_body(c, carry, s=s):
            r0 = pl.multiple_of(c * tq, tq)
            row0 = pl.multiple_of(s * tm + c * tq, tq)
            if prompt:
                first_valid = WINDOW - (t * tm + c * tq)
            for kvh in range(N_KV):
                q4 = jnp.concatenate(
                    [qn_ref[pl.ds(row0, tq), (kvh * GROUP + g_) * HEAD_DIM:(kvh * GROUP + g_ + 1) * HEAD_DIM]
                     for g_ in range(GROUP)], axis=0)
                ks = slice(kvh * HEAD_DIM, (kvh + 1) * HEAD_DIM)
                kb = kbuf_ref[s, pl.ds(r0, nk), ks]
                vb = vbuf_ref[s, pl.ds(r0, nk), ks]
                sc = lax.dot_general(q4, kb, (((1,), (1,)), ((), ())), preferred_element_type=F32)
                sc = sc + bias_ref[kvh]
                if prompt:
                    sc = jnp.where(key_col >= first_valid, sc, NEG)
                sink = sink_cols[kvh]
                mx = jnp.maximum(jnp.max(sc, axis=-1, keepdims=True), sink)
                p = jnp.exp(sc - mx)
                denom = jnp.sum(p, axis=-1, keepdims=True) + jnp.exp(sink - mx)
                o = _dot(p.astype(BF16), vb) / denom
                o_cat = jnp.concatenate([o[g_ * tq:(g_ + 1) * tq] for g_ in range(GROUP)], axis=1)
                ao_ref[pl.ds(row0, tq), kvh * GROUP * HEAD_DIM:(kvh + 1) * GROUP * HEAD_DIM] = o_cat.astype(BF16)
            return carry
        lax.fori_loop(0, n_chunks, chunk_body, 0)

    if multi_step:
        kbuf_ref[:, :WINDOW, :] = kbuf_ref[:, tm:, :]
        vbuf_ref[:, :WINDOW, :] = vbuf_ref[:, tm:, :]

    x1 = x + _dot(ao_ref[...], wo_ref[...])
    y_ref[...] = _ffn(x1, gf_ref, wg_ref, wu_ref, wd_ref, act_ref).reshape(ns, tm, D_MODEL)


def _resident(arr):
    nd = arr.ndim
    return pl.BlockSpec(arr.shape, lambda b, t, _nd=nd: (0,) * _nd, pipeline_mode=pl.Buffered(1))


def _params():
    return pltpu.CompilerParams(dimension_semantics=("arbitrary", "arbitrary"),
                                vmem_limit_bytes=VMEM_LIMIT)


def _layer0(x, state, w, *, ns, tm, name):
    nb, seq, _ = x.shape
    grid = (nb // ns, seq // tm)
    m = ns * tm
    weights = [w["gm"], w["wpw1"], w["bpw1"], w["wdw"], w["bdw"], w["lng"], w["lnb"], w["wpw2"],
               w["bpw2"], w["gf"], w["wg"], w["wu"], w["wd"]]
    return pl.pallas_call(
        functools.partial(_layer0_kernel, ns=ns, tm=tm, multi_step=grid[1] > 1),
        grid=grid,
        in_specs=[pl.BlockSpec((ns, tm, D_MODEL), lambda b, t: (b, t, 0)),
                  pl.BlockSpec((ns, HALO, D_MODEL), lambda b, t: (b, 0, 0))]
                 + [_resident(a) for a in weights],
        out_specs=[pl.BlockSpec((ns, tm, D_MODEL), lambda b, t: (b, t, 0)),
                   pl.BlockSpec((ns, HALO, D_MODEL), lambda b, t: (b, 0, 0))],
        out_shape=[jax.ShapeDtypeStruct((nb, seq, D_MODEL), F32),
                   jax.ShapeDtypeStruct((nb, HALO, D_MODEL), F32)],
        scratch_shapes=[pltpu.VMEM((ns, HALO + tm, D_MODEL), F32),
                        pltpu.VMEM((m, D_MODEL), F32),
                        pltpu.VMEM((m, D_FF), BF16)],
        compiler_params=_params(),
        name=name,
    )(x, state, *weights)


def _layer1(x, caches, w, *, ns, tm, tq, name):
    nb, seq, _ = x.shape
    grid = (nb // ns, seq // tm)
    m = ns * tm
    prompt = caches is None
    weights = [w["gm"], w["wqkv"], w["qg"], w["kg"], w["wo"], w["gf"], w["wg"], w["wu"], w["wd"]]
    cache_spec = pl.BlockSpec((ns, WINDOW, KV_W), lambda b, t: (b, 0, 0))
    in_specs = [pl.BlockSpec((ns, tm, D_MODEL), lambda b, t: (b, t, 0))]
    args = [x]
    if not prompt:
        in_specs += [cache_spec, cache_spec]
        args += list(caches)
    in_specs += [pl.BlockSpec(memory_space=pltpu.SMEM)] + [_resident(a) for a in weights]
    args += [w["sinks"]] + weights
    return pl.pallas_call(
        functools.partial(_layer1_kernel, ns=ns, tm=tm, tq=tq, prompt=prompt, multi_step=grid[1] > 1),
        grid=grid,
        in_specs=in_specs,
        out_specs=[pl.BlockSpec((ns, tm, D_MODEL), lambda b, t: (b, t, 0)), cache_spec, cache_spec],
        out_shape=[jax.ShapeDtypeStruct((nb, seq, D_MODEL), F32),
                   jax.ShapeDtypeStruct((nb, WINDOW, KV_W), F32),
                   jax.ShapeDtypeStruct((nb, WINDOW, KV_W), F32)],
        scratch_shapes=[pltpu.VMEM((m, D_MODEL), BF16),
                        pltpu.VMEM((ns, WINDOW + tm, KV_W), BF16),
                        pltpu.VMEM((ns, WINDOW + tm, KV_W), BF16),
                        pltpu.VMEM((m, D_MODEL), BF16),
                        pltpu.VMEM((m, D_FF), BF16),
                        pltpu.VMEM((N_KV, GROUP * tq, WINDOW + tq), F32)],
        compiler_params=_params(),
        name=name,
    )(*args)


PROMPT_TM = 256


def kernel(x_prompt, x_sample, state_conv, cache_k, cache_v, g_mix, g_ffn, w_pw1, b_pw1, w_dw, b_dw,
           ln_g, ln_b, w_pw2, b_pw2, w_qkv, qn_g, kn_g, sinks, w_o, w_gate, w_up, w_down):
    nb = x_prompt.shape[0]
    ndb, dec_seq, _ = x_sample.shape
    row = lambda a: a.reshape(1, -1)
    w0 = dict(gm=row(g_mix[0]), wpw1=w_pw1[0].astype(BF16), bpw1=row(b_pw1[0]), wdw=w_dw[0],
              bdw=row(b_dw[0]), lng=row(ln_g[0]), lnb=row(ln_b[0]), wpw2=w_pw2[0].astype(BF16),
              bpw2=row(b_pw2[0]), gf=row(g_ffn[0]), wg=w_gate[0].astype(BF16),
              wu=w_up[0].astype(BF16), wd=w_down[0].astype(BF16))
    w1 = dict(gm=row(g_mix[1]), wqkv=w_qkv[0].astype(BF16), qg=row(jnp.tile(qn_g[0], N_HEADS)),
              kg=row(jnp.tile(kn_g[0], N_KV)), wo=w_o[0].astype(BF16), gf=row(g_ffn[1]),
              wg=w_gate[1].astype(BF16), wu=w_up[1].astype(BF16), wd=w_down[1].astype(BF16),
              sinks=sinks[0])

    pad_state = lambda st: jnp.pad(st, ((0, 0), (HALO_PAD, 0), (0, 0)))
    zero_state = jnp.zeros((nb, HALO, D_MODEL), F32)
    xp1, cst_p = _layer0(x_prompt, zero_state, w0, ns=1, tm=PROMPT_TM, name="layer0_prompt")
    xs1, cst_s = _layer0(x_sample, pad_state(state_conv[0]), w0, ns=ndb, tm=dec_seq, name="layer0_sample")

    kc = cache_k[0].reshape(ndb, WINDOW, KV_W)
    vc = cache_v[0].reshape(ndb, WINDOW, KV_W)
    yp, kp, vp = _layer1(xp1, None, w1, ns=1, tm=PROMPT_TM, tq=CHUNK, name="layer1_prompt")
    ys, ks, vs = _layer1(xs1, (kc, vc), w1, ns=ndb, tm=dec_seq, tq=dec_seq, name="layer1_sample")

    kv_shape = lambda a: a.reshape(1, a.shape[0], WINDOW, N_KV, HEAD_DIM)
    return (yp, ys, cst_p[None, :, HALO_PAD:, :], cst_s[None, :, HALO_PAD:, :],
            kv_shape(kp), kv_shape(vp), kv_shape(ks), kv_shape(vs))
```

```python
import functools

import jax
import jax.numpy as jnp
from jax import lax
from jax.experimental import pallas as pl
from jax.experimental.pallas import tpu as pltpu

D_MODEL = 1024
CHUNK = 64
CONV_W = 31
HEAD_DIM = 64
N_HEADS = D_MODEL // HEAD_DIM
N_KV = 4
GROUP = N_HEADS // N_KV
WINDOW = 128
D_FF = 2816
QKV_W = (N_HEADS + 2 * N_KV) * HEAD_DIM
KV_W = N_KV * HEAD_DIM
SCALE = HEAD_DIM ** -0.5
NEG = -1e30
RMS_EPS = 1e-6
LN_EPS = 1e-5
PAST_LEN = 2048

SUBLANES = 8
HALO = -(-(CONV_W - 1) // SUBLANES) * SUBLANES
HALO_PAD = HALO - (CONV_W - 1)
MXU_N = 256
CONV_ROWS = 64
CONV_COLS = 128
VMEM_LIMIT = 56 * 1024 * 1024

F32 = jnp.float32
BF16 = jnp.bfloat16


def _dot(a, b):
    return jnp.dot(a, b, preferred_element_type=F32)


def _rms(x, g):
    ms = jnp.mean(x * x, axis=-1, keepdims=True)
    return x * lax.rsqrt(ms + RMS_EPS) * g


def _ffn(x1, gf_ref, wg_ref, wu_ref, wd_ref, act_ref):
    hn = _rms(x1, gf_ref[...]).astype(BF16)
    for j in range(D_FF // MXU_N):
        sl = slice(j * MXU_N, (j + 1) * MXU_N)
        gt = _dot(hn, wg_ref[:, sl])
        up = _dot(hn, wu_ref[:, sl])
        act_ref[:, sl] = (gt * jax.nn.sigmoid(gt) * up).astype(BF16)
    return x1 + _dot(act_ref[...], wd_ref[...])


def _layer0_kernel(x_ref, st_ref, gm_ref, wpw1_ref, bpw1_ref, wdw_ref, bdw_ref, lng_ref, lnb_ref,
                   wpw2_ref, bpw2_ref, gf_ref, wg_ref, wu_ref, wd_ref,
                   y_ref, nst_ref, gext_ref, cbuf_ref, act_ref, *, ns, tm, multi_step):
    m = ns * tm
    t = pl.program_id(1)

    @pl.when(t == 0)
    def _():
        gext_ref[:, :HALO, :] = st_ref[...]

    x = x_ref[...].reshape(m, D_MODEL)
    h = _rms(x, gm_ref[...]).astype(BF16)
    a = _dot(h, wpw1_ref[:, :D_MODEL]) + bpw1_ref[:, :D_MODEL]
    gate = _dot(h, wpw1_ref[:, D_MODEL:]) + bpw1_ref[:, D_MODEL:]
    g = a * jax.nn.sigmoid(gate)
    gext_ref[:, HALO:, :] = g.reshape(ns, tm, D_MODEL)

    rb = min(tm, CONV_ROWS)
    nz = rb + HALO
    for s in range(ns):
        for r0 in range(0, tm, rb):
            for c0 in range(0, D_MODEL, CONV_COLS):
                cs = slice(c0, c0 + CONV_COLS)
                z = gext_ref[s, pl.ds(r0, nz), cs]
                acc = None
                for p in range(SUBLANES):
                    zp = z if p == 0 else pltpu.roll(z, nz - p, axis=0)
                    for k in range(CONV_W):
                        if (HALO_PAD + k) % SUBLANES != p:
                            continue
                        off = HALO_PAD + k - p
                        term = zp[off:off + rb] * wdw_ref[k:k + 1, cs]
                        acc = term if acc is None else acc + term
                cbuf_ref[pl.ds(s * tm + r0, rb), cs] = acc + bdw_ref[:, cs]

    nst_ref[...] = gext_ref[:, tm:, :]
    if multi_step:
        gext_ref[:, :HALO, :] = gext_ref[:, tm:, :]

    c = cbuf_ref[...]
    mu = jnp.mean(c, axis=-1, keepdims=True)
    xc = c - mu
    ln = xc * lax.rsqrt(jnp.mean(xc * xc, axis=-1, keepdims=True) + LN_EPS) * lng_ref[...] + lnb_ref[...]
    sw = (ln * jax.nn.sigmoid(ln)).astype(BF16)
    x1 = x + _dot(sw, wpw2_ref[...]) + bpw2_ref[...]
    y_ref[...] = _ffn(x1, gf_ref, wg_ref, wu_ref, wd_ref, act_ref).reshape(ns, tm, D_MODEL)


def _head_mean_square(z):
    r = lax.broadcasted_iota(jnp.int32, (MXU_N, MXU_N), 0) // HEAD_DIM
    c = lax.broadcasted_iota(jnp.int32, (MXU_N, MXU_N), 1) // HEAD_DIM
    ones_bd = jnp.where(r == c, 1.0, 0.0).astype(BF16)
    outs = []
    for j in range(z.shape[1] // MXU_N):
        zz = z[:, j * MXU_N:(j + 1) * MXU_N]
        zz = zz * zz
        hi = zz.astype(BF16)
        lo = (zz - hi.astype(F32)).astype(BF16)
        outs.append(_dot(hi, ones_bd) + _dot(lo, ones_bd))
    ss = outs[0] if len(outs) == 1 else jnp.concatenate(outs, axis=1)
    return ss * (1.0 / HEAD_DIM)


def _layer1_kernel(*refs, ns, tm, tq, prompt, multi_step):
    if prompt:
        (x_ref, sinks_ref, gm_ref, wqkv_ref, qg_ref, kg_ref, wo_ref, gf_ref, wg_ref, wu_ref, wd_ref,
         y_ref, ko_ref, vo_ref, qn_ref, kbuf_ref, vbuf_ref, ao_ref, act_ref, bias_ref) = refs
        kc_ref = vc_ref = None
    else:
        (x_ref, kc_ref, vc_ref, sinks_ref, gm_ref, wqkv_ref, qg_ref, kg_ref, wo_ref, gf_ref, wg_ref,
         wu_ref, wd_ref, y_ref, ko_ref, vo_ref, qn_ref, kbuf_ref, vbuf_ref, ao_ref, act_ref,
         bias_ref) = refs
    m = ns * tm
    nk = WINDOW + tq
    rows4 = GROUP * tq
    b = pl.program_id(0)
    t = pl.program_id(1)

    @pl.when((b == 0) & (t == 0))
    def _():
        rows = lax.broadcasted_iota(jnp.int32, (rows4, nk), 0)
        cols = lax.broadcasted_iota(jnp.int32, (rows4, nk), 1)
        dist = jnp.abs((rows % tq) + WINDOW - cols).astype(F32)
        grp = rows // tq
        for kvh in range(N_KV):
            slope = jnp.zeros((rows4, nk), F32)
            for g_ in range(GROUP):
                hh = kvh * GROUP + g_
                slope = jnp.where(grp == g_, 2.0 ** (-8.0 * (hh + 1) / N_HEADS), slope)
            bias_ref[kvh] = -(slope * dist)

    @pl.when(t == 0)
    def _():
        if prompt:
            kbuf_ref[:, :WINDOW, :] = jnp.zeros((ns, WINDOW, KV_W), BF16)
            vbuf_ref[:, :WINDOW, :] = jnp.zeros((ns, WINDOW, KV_W), BF16)
        else:
            kbuf_ref[:, :WINDOW, :] = kc_ref[...].astype(BF16)
            vbuf_ref[:, :WINDOW, :] = vc_ref[...].astype(BF16)

    x = x_ref[...].reshape(m, D_MODEL)
    h = _rms(x, gm_ref[...]).astype(BF16)
    q = _dot(h, wqkv_ref[:, :D_MODEL])
    k = _dot(h, wqkv_ref[:, D_MODEL:D_MODEL + KV_W])
    v = _dot(h, wqkv_ref[:, D_MODEL + KV_W:])
    qn = q * lax.rsqrt(_head_mean_square(q) + RMS_EPS) * (qg_ref[...] * SCALE)
    kn = k * lax.rsqrt(_head_mean_square(k) + RMS_EPS) * kg_ref[...]
    qn_ref[...] = qn.astype(BF16)
    kbuf_ref[:, WINDOW:, :] = kn.astype(BF16).reshape(ns, tm, KV_W)
    vbuf_ref[:, WINDOW:, :] = v.astype(BF16).reshape(ns, tm, KV_W)
    kn3 = kn.reshape(ns, tm, KV_W)
    v3 = v.reshape(ns, tm, KV_W)
    if tm >= WINDOW:
        ko_ref[...] = kn3[:, tm - WINDOW:, :]
        vo_ref[...] = v3[:, tm - WINDOW:, :]
    else:
        ko_ref[:, :WINDOW - tm, :] = kc_ref[:, tm:, :]
        vo_ref[:, :WINDOW - tm, :] = vc_ref[:, tm:, :]
        ko_ref[:, WINDOW - tm:, :] = kn3
        vo_ref[:, WINDOW - tm:, :] = v3

    grp_col = lax.broadcasted_iota(jnp.int32, (rows4, 1), 0) // tq
    sink_cols = []
    for kvh in range(N_KV):
        sc = jnp.zeros((rows4, 1), F32)
        for g_ in range(GROUP):
            sc = jnp.where(grp_col == g_, sinks_ref[kvh * GROUP + g_], sc)
        sink_cols.append(sc)
    key_col = lax.broadcasted_iota(jnp.int32, (rows4, nk), 1)
    n_chunks = tm // tq

    for s in range(ns):
        def chunk_body(c, carry, s=s):
            r0 = pl.multiple_of(c * tq, tq)
            row0 = pl.multiple_of(s * tm + c * tq, tq)
            if prompt:
                first_valid = WINDOW - (t * tm + c * tq)
            for kvh in range(N_KV):
                q4 = jnp.concatenate(
                    [qn_ref[pl.ds(row0, tq), (kvh * GROUP + g_) * HEAD_DIM:(kvh * GROUP + g_ + 1) * HEAD_DIM]
                     for g_ in range(GROUP)], axis=0)
                ks = slice(kvh * HEAD_DIM, (kvh + 1) * HEAD_DIM)
                kb = kbuf_ref[s, pl.ds(r0, nk), ks]
                vb = vbuf_ref[s, pl.ds(r0, nk), ks]
                sc = lax.dot_general(q4, kb, (((1,), (1,)), ((), ())), preferred_element_type=F32)
                sc = sc + bias_ref[kvh]
                if prompt:
                    sc = jnp.where(key_col >= first_valid, sc, NEG)
                sink = sink_cols[kvh]
                mx = jnp.maximum(jnp.max(sc, axis=-1, keepdims=True), sink)
                p = jnp.exp(sc - mx)
                denom = jnp.sum(p, axis=-1, keepdims=True) + jnp.exp(sink - mx)
                o = _dot(p.astype(BF16), vb) / denom
                o_cat = jnp.concatenate([o[g_ * tq:(g_ + 1) * tq] for g_ in range(GROUP)], axis=1)
                ao_ref[pl.ds(row0, tq), kvh * GROUP * HEAD_DIM:(kvh + 1) * GROUP * HEAD_DIM] = o_cat.astype(BF16)
            return carry
        lax.fori_loop(0, n_chunks, chunk_body, 0)

    if multi_step:
        kbuf_ref[:, :WINDOW, :] = kbuf_ref[:, tm:, :]
        vbuf_ref[:, :WINDOW, :] = vbuf_ref[:, tm:, :]

    x1 = x + _dot(ao_ref[...], wo_ref[...])
    y_ref[...] = _ffn(x1, gf_ref, wg_ref, wu_ref, wd_ref, act_ref).reshape(ns, tm, D_MODEL)


def _resident(arr):
    nd = arr.ndim
    return pl.BlockSpec(arr.shape, lambda b, t, _nd=nd: (0,) * _nd, pipeline_mode=pl.Buffered(1))


def _params():
    return pltpu.CompilerParams(dimension_semantics=("arbitrary", "arbitrary"),
                                vmem_limit_bytes=VMEM_LIMIT)


def _layer0(x, state, w, *, ns, tm, name):
    nb, seq, _ = x.shape
    grid = (nb // ns, seq // tm)
    m = ns * tm
    weights = [w["gm"], w["wpw1"], w["bpw1"], w["wdw"], w["bdw"], w["lng"], w["lnb"], w["wpw2"],
               w["bpw2"], w["gf"], w["wg"], w["wu"], w["wd"]]
    return pl.pallas_call(
        functools.partial(_layer0_kernel, ns=ns, tm=tm, multi_step=grid[1] > 1),
        grid=grid,
        in_specs=[pl.BlockSpec((ns, tm, D_MODEL), lambda b, t: (b, t, 0)),
                  pl.BlockSpec((ns, HALO, D_MODEL), lambda b, t: (b, 0, 0))]
                 + [_resident(a) for a in weights],
        out_specs=[pl.BlockSpec((ns, tm, D_MODEL), lambda b, t: (b, t, 0)),
                   pl.BlockSpec((ns, HALO, D_MODEL), lambda b, t: (b, 0, 0))],
        out_shape=[jax.ShapeDtypeStruct((nb, seq, D_MODEL), F32),
                   jax.ShapeDtypeStruct((nb, HALO, D_MODEL), F32)],
        scratch_shapes=[pltpu.VMEM((ns, HALO + tm, D_MODEL), F32),
                        pltpu.VMEM((m, D_MODEL), F32),
                        pltpu.VMEM((m, D_FF), BF16)],
        compiler_params=_params(),
        name=name,
    )(x, state, *weights)


def _layer1(x, caches, w, *, ns, tm, tq, name):
    nb, seq, _ = x.shape
    grid = (nb // ns, seq // tm)
    m = ns * tm
    prompt = caches is None
    weights = [w["gm"], w["wqkv"], w["qg"], w["kg"], w["wo"], w["gf"], w["wg"], w["wu"], w["wd"]]
    cache_spec = pl.BlockSpec((ns, WINDOW, KV_W), lambda b, t: (b, 0, 0))
    in_specs = [pl.BlockSpec((ns, tm, D_MODEL), lambda b, t: (b, t, 0))]
    args = [x]
    if not prompt:
        in_specs += [cache_spec, cache_spec]
        args += list(caches)
    in_specs += [pl.BlockSpec(memory_space=pltpu.SMEM)] + [_resident(a) for a in weights]
    args += [w["sinks"]] + weights
    return pl.pallas_call(
        functools.partial(_layer1_kernel, ns=ns, tm=tm, tq=tq, prompt=prompt, multi_step=grid[1] > 1),
        grid=grid,
        in_specs=in_specs,
        out_specs=[pl.BlockSpec((ns, tm, D_MODEL), lambda b, t: (b, t, 0)), cache_spec, cache_spec],
        out_shape=[jax.ShapeDtypeStruct((nb, seq, D_MODEL), F32),
                   jax.ShapeDtypeStruct((nb, WINDOW, KV_W), F32),
                   jax.ShapeDtypeStruct((nb, WINDOW, KV_W), F32)],
        scratch_shapes=[pltpu.VMEM((m, D_MODEL), BF16),
                        pltpu.VMEM((ns, WINDOW + tm, KV_W), BF16),
                        pltpu.VMEM((ns, WINDOW + tm, KV_W), BF16),
                        pltpu.VMEM((m, D_MODEL), BF16),
                        pltpu.VMEM((m, D_FF), BF16),
                        pltpu.VMEM((N_KV, GROUP * tq, WINDOW + tq), F32)],
        compiler_params=_params(),
        name=name,
    )(*args)


PROMPT_TM = 512


def kernel(x_prompt, x_sample, state_conv, cache_k, cache_v, g_mix, g_ffn, w_pw1, b_pw1, w_dw, b_dw,
           ln_g, ln_b, w_pw2, b_pw2, w_qkv, qn_g, kn_g, sinks, w_o, w_gate, w_up, w_down):
    nb = x_prompt.shape[0]
    ndb, dec_seq, _ = x_sample.shape
    row = lambda a: a.reshape(1, -1)
    w0 = dict(gm=row(g_mix[0]), wpw1=w_pw1[0].astype(BF16), bpw1=row(b_pw1[0]), wdw=w_dw[0],
              bdw=row(b_dw[0]), lng=row(ln_g[0]), lnb=row(ln_b[0]), wpw2=w_pw2[0].astype(BF16),
              bpw2=row(b_pw2[0]), gf=row(g_ffn[0]), wg=w_gate[0].astype(BF16),
              wu=w_up[0].astype(BF16), wd=w_down[0].astype(BF16))
    w1 = dict(gm=row(g_mix[1]), wqkv=w_qkv[0].astype(BF16), qg=row(jnp.tile(qn_g[0], N_HEADS)),
              kg=row(jnp.tile(kn_g[0], N_KV)), wo=w_o[0].astype(BF16), gf=row(g_ffn[1]),
              wg=w_gate[1].astype(BF16), wu=w_up[1].astype(BF16), wd=w_down[1].astype(BF16),
              sinks=sinks[0])

    pad_state = lambda st: jnp.pad(st, ((0, 0), (HALO_PAD, 0), (0, 0)))
    zero_state = jnp.zeros((nb, HALO, D_MODEL), F32)
    xp1, cst_p = _layer0(x_prompt, zero_state, w0, ns=1, tm=PROMPT_TM, name="layer0_prompt")
    xs1, cst_s = _layer0(x_sample, pad_state(state_conv[0]), w0, ns=ndb, tm=dec_seq, name="layer0_sample")

    kc = cache_k[0].reshape(ndb, WINDOW, KV_W)
    vc = cache_v[0].reshape(ndb, WINDOW, KV_W)
    yp, kp, vp = _layer1(xp1, None, w1, ns=1, tm=PROMPT_TM, tq=CHUNK, name="layer1_prompt")
    ys, ks, vs = _layer1(xs1, (kc, vc), w1, ns=ndb, tm=dec_seq, tq=dec_seq, name="layer1_sample")

    kv_shape = lambda a: a.reshape(1, a.shape[0], WINDOW, N_KV, HEAD_DIM)
    return (yp, ys, cst_p[None, :, HALO_PAD:, :], cst_s[None, :, HALO_PAD:, :],
            kv_shape(kp), kv_shape(vp), kv_shape(ks), kv_shape(vs))
```

```python
import functools

import jax
import jax.numpy as jnp
from jax import lax
from jax.experimental import pallas as pl
from jax.experimental.pallas import tpu as pltpu

D_MODEL = 1024
CHUNK = 64
CONV_W = 31
HEAD_DIM = 64
N_HEADS = D_MODEL // HEAD_DIM
N_KV = 4
GROUP = N_HEADS // N_KV
WINDOW = 128
D_FF = 2816
QKV_W = (N_HEADS + 2 * N_KV) * HEAD_DIM
KV_W = N_KV * HEAD_DIM
SCALE = HEAD_DIM ** -0.5
NEG = -1e30
RMS_EPS = 1e-6
LN_EPS = 1e-5

SUBLANES = 8
HALO = -(-(CONV_W - 1) // SUBLANES) * SUBLANES
HALO_PAD = HALO - (CONV_W - 1)
MXU_N = 256
CONV_ROWS = 64
CONV_COLS = 128
LN_ROWS = 64
VMEM_LIMIT = 56 * 1024 * 1024
PROMPT_TM = 256

F32 = jnp.float32
BF16 = jnp.bfloat16


def _dot(a, b):
    return jnp.dot(a, b, preferred_element_type=F32)


def _rms(x, g):
    ms = jnp.mean(x * x, axis=-1, keepdims=True)
    return x * lax.rsqrt(ms + RMS_EPS) * g


def _run(gen):
    for _ in gen:
        pass


def _interleave(main, n_main, side, n_side):
    next(side)
    done = 1
    for i in range(n_main):
        tok = next(main)
        want = max(1, (n_side * (i + 1)) // n_main)
        for _ in range(want - done):
            side.send(tok)
        done = want
    _run(main)
    _run(side)


def _order_after(z, tok):
    if tok is None:
        return z
    bits = pltpu.bitcast(tok, jnp.uint32)
    bits = lax.shift_right_logical(lax.shift_right_logical(bits, jnp.uint32(16)), jnp.uint32(16))
    zero = pltpu.bitcast(bits, F32)
    return z + jnp.concatenate([zero] * (z.shape[0] // zero.shape[0]), axis=0)


FFN_ITEMS = D_FF // MXU_N + D_MODEL // MXU_N


def _ffn_items(x1, gf_ref, wg_ref, wu_ref, wd_ref, act_ref, y_ref):
    hn = _rms(x1, gf_ref[...]).astype(BF16)
    for j in range(D_FF // MXU_N):
        sl = slice(j * MXU_N, (j + 1) * MXU_N)
        gt = _dot(hn, wg_ref[:, sl])
        up = _dot(hn, wu_ref[:, sl])
        act_ref[:, sl] = (gt * jax.nn.sigmoid(gt) * up).astype(BF16)
        yield gt[:SUBLANES, :128]
    for j in range(D_MODEL // MXU_N):
        sl = slice(j * MXU_N, (j + 1) * MXU_N)
        dn = _dot(act_ref[...], wd_ref[:, sl])
        y_ref[:, sl] = x1[:, sl] + dn
        yield dn[:SUBLANES, :128]


def _l0_mixer_items(x, w, gext_ref, cbuf_ref, sw_ref, *, ns, tm, carry):
    (gm_ref, wpw1_ref, bpw1_ref, wdw_ref, bdw_ref, lng_ref, lnb_ref) = w
    m = ns * tm
    h = _rms(x, gm_ref[...]).astype(BF16)
    a = _dot(h, wpw1_ref[:, :D_MODEL]) + bpw1_ref[:, :D_MODEL]
    gate = _dot(h, wpw1_ref[:, D_MODEL:]) + bpw1_ref[:, D_MODEL:]
    g = a * jax.nn.sigmoid(gate)
    gext_ref[:, HALO:, :] = g.reshape(ns, tm, D_MODEL)
    tok = yield

    rb = min(tm, CONV_ROWS)
    nz = rb + HALO
    for s in range(ns):
        for r0 in range(0, tm, rb):
            for c0 in range(0, D_MODEL, CONV_COLS):
                cs = slice(c0, c0 + CONV_COLS)
                z = _order_after(gext_ref[s, pl.ds(r0, nz), cs], tok)
                acc = None
                for p in range(SUBLANES):
                    zp = z if p == 0 else pltpu.roll(z, nz - p, axis=0)
                    for k in range(CONV_W):
                        if (HALO_PAD + k) % SUBLANES != p:
                            continue
                        off = HALO_PAD + k - p
                        term = zp[off:off + rb] * wdw_ref[k:k + 1, cs]
                        acc = term if acc is None else acc + term
                cbuf_ref[pl.ds(s * tm + r0, rb), cs] = acc + bdw_ref[:, cs]
                tok = yield
    if carry:
        gext_ref[:, :HALO, :] = gext_ref[:, tm:, :]

    lr = min(m, LN_ROWS)
    for r0 in range(0, m, lr):
        c = cbuf_ref[pl.ds(r0, lr), :]
        mu = jnp.mean(c, axis=-1, keepdims=True)
        xc = c - mu
        ln = xc * lax.rsqrt(jnp.mean(xc * xc, axis=-1, keepdims=True) + LN_EPS) * lng_ref[...] + lnb_ref[...]
        sw_ref[pl.ds(r0, lr), :] = (ln * jax.nn.sigmoid(ln)).astype(BF16)
        yield


def _l0_mixer_count(ns, tm):
    rb = min(tm, CONV_ROWS)
    return 1 + ns * (tm // rb) * (D_MODEL // CONV_COLS) + (ns * tm) // min(ns * tm, LN_ROWS)


def _l0_ffn_items(xres, sw_ref, w, act_ref, y_ref):
    (wpw2_ref, bpw2_ref, gf_ref, wg_ref, wu_ref, wd_ref) = w
    x1 = xres + _dot(sw_ref[...], wpw2_ref[...]) + bpw2_ref[...]
    yield x1[:SUBLANES, :128]
    yield from _ffn_items(x1, gf_ref, wg_ref, wu_ref, wd_ref, act_ref, y_ref)


def _layer0_prompt_kernel(x_ref, xprev_ref, st_ref, gm_ref, wpw1_ref, bpw1_ref, wdw_ref, bdw_ref,
                          lng_ref, lnb_ref, wpw2_ref, bpw2_ref, gf_ref, wg_ref, wu_ref, wd_ref,
                          y_ref, nst_ref, gext_ref, cbuf_ref, sw_ref, act_ref, *, tm, n_t, n_tiles):
    i = pl.program_id(0)
    t = lax.rem(jnp.minimum(i, n_tiles - 1), n_t)

    @pl.when(i == 0)
    def _():
        sw_ref[...] = jnp.zeros(sw_ref.shape, BF16)

    @pl.when(t == 0)
    def _():
        gext_ref[:, :HALO, :] = st_ref[...]

    ffn = _l0_ffn_items(xprev_ref[0], sw_ref, (wpw2_ref, bpw2_ref, gf_ref, wg_ref, wu_ref, wd_ref),
                        act_ref, y_ref.at[0])
    mixer = _l0_mixer_items(x_ref[0], (gm_ref, wpw1_ref, bpw1_ref, wdw_ref, bdw_ref, lng_ref, lnb_ref),
                            gext_ref, cbuf_ref, sw_ref, ns=1, tm=tm, carry=True)
    _interleave(ffn, 1 + FFN_ITEMS, mixer, _l0_mixer_count(1, tm))

    @pl.when((t == n_t - 1) & (i < n_tiles))
    def _():
        nst_ref[...] = gext_ref[:, tm:, :]


def _layer0_sample_kernel(x_ref, st_ref, gm_ref, wpw1_ref, bpw1_ref, wdw_ref, bdw_ref, lng_ref, lnb_ref,
                          wpw2_ref, bpw2_ref, gf_ref, wg_ref, wu_ref, wd_ref,
                          y_ref, nst_ref, gext_ref, cbuf_ref, sw_ref, act_ref, yflat_ref, *, ns, tm):
    m = ns * tm
    gext_ref[:, :HALO, :] = st_ref[...]
    x = x_ref[...].reshape(m, D_MODEL)
    _run(_l0_mixer_items(x, (gm_ref, wpw1_ref, bpw1_ref, wdw_ref, bdw_ref, lng_ref, lnb_ref),
                         gext_ref, cbuf_ref, sw_ref, ns=ns, tm=tm, carry=False))
    nst_ref[...] = gext_ref[:, tm:, :]
    _run(_l0_ffn_items(x, sw_ref, (wpw2_ref, bpw2_ref, gf_ref, wg_ref, wu_ref, wd_ref), act_ref, yflat_ref))
    y_ref[...] = yflat_ref[...].reshape(ns, tm, D_MODEL)


def _head_mean_square(z):
    r = lax.broadcasted_iota(jnp.int32, (MXU_N, MXU_N), 0) // HEAD_DIM
    c = lax.broadcasted_iota(jnp.int32, (MXU_N, MXU_N), 1) // HEAD_DIM
    ones_bd = jnp.where(r == c, 1.0, 0.0).astype(BF16)
    outs = []
    for j in range(z.shape[1] // MXU_N):
        zz = z[:, j * MXU_N:(j + 1) * MXU_N]
        zz = zz * zz
        hi = zz.astype(BF16)
        lo = (zz - hi.astype(F32)).astype(BF16)
        outs.append(_dot(hi, ones_bd) + _dot(lo, ones_bd))
    ss = outs[0] if len(outs) == 1 else jnp.concatenate(outs, axis=1)
    return ss * (1.0 / HEAD_DIM)


def _init_alibi_bias(bias_ref, tq):
    rows4, nk = bias_ref.shape[1:]
    rows = lax.broadcasted_iota(jnp.int32, (rows4, nk), 0)
    cols = lax.broadcasted_iota(jnp.int32, (rows4, nk), 1)
    dist = jnp.abs((rows % tq) + WINDOW - cols).astype(F32)
    grp = rows // tq
    for kvh in range(N_KV):
        slope = jnp.zeros((rows4, nk), F32)
        for g_ in range(GROUP):
            slope = jnp.where(grp == g_, 2.0 ** (-8.0 * (kvh * GROUP + g_ + 1) / N_HEADS), slope)
        bias_ref[kvh] = -(slope * dist)


def _l1_mixer_items(x, w, caches, sinks_ref, ko_ref, vo_ref, qn_ref, kbuf_ref, vbuf_ref, ao_ref, bias_ref,
                    *, ns, tm, tq, first_pos, carry):
    (gm_ref, wqkv_ref, qg_ref, kg_ref) = w
    nk = WINDOW + tq
    rows4 = GROUP * tq
    h = _rms(x, gm_ref[...]).astype(BF16)
    q = _dot(h, wqkv_ref[:, :D_MODEL])
    k = _dot(h, wqkv_ref[:, D_MODEL:D_MODEL + KV_W])
    v = _dot(h, wqkv_ref[:, D_MODEL + KV_W:])
    qn = q * lax.rsqrt(_head_mean_square(q) + RMS_EPS) * (qg_ref[...] * SCALE)
    kn = k * lax.rsqrt(_head_mean_square(k) + RMS_EPS) * kg_ref[...]
    qn_ref[...] = qn.astype(BF16)
    kbuf_ref[:, WINDOW:, :] = kn.astype(BF16).reshape(ns, tm, KV_W)
    vbuf_ref[:, WINDOW:, :] = v.astype(BF16).reshape(ns, tm, KV_W)
    kn3 = kn.reshape(ns, tm, KV_W)
    v3 = v.reshape(ns, tm, KV_W)
    if tm >= WINDOW:
        ko_ref[...] = kn3[:, tm - WINDOW:, :]
        vo_ref[...] = v3[:, tm - WINDOW:, :]
    else:
        kc_ref, vc_ref = caches
        ko_ref[:, :WINDOW - tm, :] = kc_ref[:, tm:, :]
        vo_ref[:, :WINDOW - tm, :] = vc_ref[:, tm:, :]
        ko_ref[:, WINDOW - tm:, :] = kn3
        vo_ref[:, WINDOW - tm:, :] = v3
    yield

    grp_col = lax.broadcasted_iota(jnp.int32, (rows4, 1), 0) // tq
    sink_cols = []
    for kvh in range(N_KV):
        sc = jnp.zeros((rows4, 1), F32)
        for g_ in range(GROUP):
            sc = jnp.where(grp_col == g_, sinks_ref[kvh * GROUP + g_], sc)
        sink_cols.append(sc)
    key_col = lax.broadcasted_iota(jnp.int32, (rows4, nk), 1)

    for s in range(ns):
        for c in range(tm // tq):
            r0 = c * tq
            row0 = s * tm + r0
            for kvh in range(N_KV):
                q4 = jnp.concatenate(
                    [qn_ref[pl.ds(row0, tq), (kvh * GROUP + g_) * HEAD_DIM:(kvh * GROUP + g_ + 1) * HEAD_DIM]
                     for g_ in range(GROUP)], axis=0)
                ks = slice(kvh * HEAD_DIM, (kvh + 1) * HEAD_DIM)
                kb = kbuf_ref[s, pl.ds(r0, nk), ks]
                vb = vbuf_ref[s, pl.ds(r0, nk), ks]
                sc = lax.dot_general(q4, kb, (((1,), (1,)), ((), ())), preferred_element_type=F32)
                sc = sc + bias_ref[kvh]
                if first_pos is not None:
                    sc = jnp.where(key_col >= WINDOW - r0 - first_pos, sc, NEG)
                sink = sink_cols[kvh]
                mx = jnp.maximum(jnp.max(sc, axis=-1, keepdims=True), sink)
                p = jnp.exp(sc - mx)
                denom = jnp.sum(p, axis=-1, keepdims=True) + jnp.exp(sink - mx)
                o = _dot(p.astype(BF16), vb) / denom
                o_cat = jnp.concatenate([o[g_ * tq:(g_ + 1) * tq] for g_ in range(GROUP)], axis=1)
                ao_ref[pl.ds(row0, tq), kvh * GROUP * HEAD_DIM:(kvh + 1) * GROUP * HEAD_DIM] = o_cat.astype(BF16)
                yield
    if carry:
        kbuf_ref[:, :WINDOW, :] = kbuf_ref[:, tm:, :]
        vbuf_ref[:, :WINDOW, :] = vbuf_ref[:, tm:, :]


def _l1_mixer_count(ns, tm, tq):
    return 1 + ns * (tm // tq) * N_KV


def _l1_ffn_items(xres, ao_ref, w, act_ref, y_ref):
    (wo_ref, gf_ref, wg_ref, wu_ref, wd_ref) = w
    x1 = xres + _dot(ao_ref[...], wo_ref[...])
    yield x1[:SUBLANES, :128]
    yield from _ffn_items(x1, gf_ref, wg_ref, wu_ref, wd_ref, act_ref, y_ref)


def _layer1_prompt_kernel(x_ref, xprev_ref, sinks_ref, gm_ref, wqkv_ref, qg_ref, kg_ref, wo_ref, gf_ref,
                          wg_ref, wu_ref, wd_ref, y_ref, ko_ref, vo_ref,
                          qn_ref, kbuf_ref, vbuf_ref, ao_ref, act_ref, bias_ref, *, tm, n_t, n_tiles):
    i = pl.program_id(0)
    t = lax.rem(jnp.minimum(i, n_tiles - 1), n_t)

    @pl.when(i == 0)
    def _():
        _init_alibi_bias(bias_ref, CHUNK)
        ao_ref[...] = jnp.zeros(ao_ref.shape, BF16)

    @pl.when(t == 0)
    def _():
        kbuf_ref[:, :WINDOW, :] = jnp.zeros((1, WINDOW, KV_W), BF16)
        vbuf_ref[:, :WINDOW, :] = jnp.zeros((1, WINDOW, KV_W), BF16)

    ffn = _l1_ffn_items(xprev_ref[0], ao_ref, (wo_ref, gf_ref, wg_ref, wu_ref, wd_ref), act_ref, y_ref.at[0])
    mixer = _l1_mixer_items(x_ref[0], (gm_ref, wqkv_ref, qg_ref, kg_ref), None, sinks_ref, ko_ref, vo_ref,
                            qn_ref, kbuf_ref, vbuf_ref, ao_ref, bias_ref,
                            ns=1, tm=tm, tq=CHUNK, first_pos=t * tm, carry=True)
    _interleave(ffn, 1 + FFN_ITEMS, mixer, _l1_mixer_count(1, tm, CHUNK))


def _layer1_sample_kernel(x_ref, kc_ref, vc_ref, sinks_ref, gm_ref, wqkv_ref, qg_ref, kg_ref, wo_ref, gf_ref,
                          wg_ref, wu_ref, wd_ref, y_ref, ko_ref, vo_ref,
                          qn_ref, kbuf_ref, vbuf_ref, ao_ref, act_ref, bias_ref, yflat_ref, *, ns, tm):
    m = ns * tm
    _init_alibi_bias(bias_ref, tm)
    kbuf_ref[:, :WINDOW, :] = kc_ref[...].astype(BF16)
    vbuf_ref[:, :WINDOW, :] = vc_ref[...].astype(BF16)
    x = x_ref[...].reshape(m, D_MODEL)
    _run(_l1_mixer_items(x, (gm_ref, wqkv_ref, qg_ref, kg_ref), (kc_ref, vc_ref), sinks_ref, ko_ref, vo_ref,
                         qn_ref, kbuf_ref, vbuf_ref, ao_ref, bias_ref,
                         ns=ns, tm=tm, tq=tm, first_pos=None, carry=False))
    _run(_l1_ffn_items(x, ao_ref, (wo_ref, gf_ref, wg_ref, wu_ref, wd_ref), act_ref, yflat_ref))
    y_ref[...] = yflat_ref[...].reshape(ns, tm, D_MODEL)


def _resident(arr):
    nd = arr.ndim
    return pl.BlockSpec(arr.shape, lambda *_, _nd=nd: (0,) * _nd, pipeline_mode=pl.Buffered(1))


def _params(n_axes):
    return pltpu.CompilerParams(dimension_semantics=("arbitrary",) * n_axes, vmem_limit_bytes=VMEM_LIMIT)


def _tile_maps(n_t, n_tiles):
    def cur(i):
        j = jnp.minimum(i, n_tiles - 1)
        return (j // n_t, j % n_t, 0)

    def prev(i):
        j = jnp.maximum(i - 1, 0)
        return (j // n_t, j % n_t, 0)

    def cur_stream(i):
        return (jnp.minimum(i, n_tiles - 1) // n_t, 0, 0)

    return cur, prev, cur_stream


def _layer0_prompt(x, state, w0, *, tm):
    nb, seq, _ = x.shape
    n_t = seq // tm
    n_tiles = nb * n_t
    cur, prev, cur_stream = _tile_maps(n_t, n_tiles)
    tile = (1, tm, D_MODEL)
    return pl.pallas_call(
        functools.partial(_layer0_prompt_kernel, tm=tm, n_t=n_t, n_tiles=n_tiles),
        grid=(n_tiles + 1,),
        in_specs=[pl.BlockSpec(tile, cur), pl.BlockSpec(tile, prev), pl.BlockSpec((1, HALO, D_MODEL), cur_stream)]
                 + [_resident(a) for a in w0],
        out_specs=[pl.BlockSpec(tile, prev), pl.BlockSpec((1, HALO, D_MODEL), cur_stream)],
        out_shape=[jax.ShapeDtypeStruct((nb, seq, D_MODEL), F32),
                   jax.ShapeDtypeStruct((nb, HALO, D_MODEL), F32)],
        scratch_shapes=[pltpu.VMEM((1, HALO + tm, D_MODEL), F32),
                        pltpu.VMEM((tm, D_MODEL), F32),
                        pltpu.VMEM((tm, D_MODEL), BF16),
                        pltpu.VMEM((tm, D_FF), BF16)],
        compiler_params=_params(1),
        name="layer0_prompt",
    )(x, x, state, *w0)


def _layer0_sample(x, state, w0):
    ns, tm, _ = x.shape
    m = ns * tm
    whole = lambda shape: pl.BlockSpec(shape, lambda i, _n=len(shape): (0,) * _n)
    return pl.pallas_call(
        functools.partial(_layer0_sample_kernel, ns=ns, tm=tm),
        grid=(1,),
        in_specs=[whole((ns, tm, D_MODEL)), whole((ns, HALO, D_MODEL))] + [_resident(a) for a in w0],
        out_specs=[whole((ns, tm, D_MODEL)), whole((ns, HALO, D_MODEL))],
        out_shape=[jax.ShapeDtypeStruct((ns, tm, D_MODEL), F32),
                   jax.ShapeDtypeStruct((ns, HALO, D_MODEL), F32)],
        scratch_shapes=[pltpu.VMEM((ns, HALO + tm, D_MODEL), F32),
                        pltpu.VMEM((m, D_MODEL), F32),
                        pltpu.VMEM((m, D_MODEL), BF16),
                        pltpu.VMEM((m, D_FF), BF16),
                        pltpu.VMEM((m, D_MODEL), F32)],
        compiler_params=_params(1),
        name="layer0_sample",
    )(x, state, *w0)


def _layer1_scratch(ns, tm, tq):
    m = ns * tm
    return [pltpu.VMEM((m, D_MODEL), BF16),
            pltpu.VMEM((ns, WINDOW + tm, KV_W), BF16),
            pltpu.VMEM((ns, WINDOW + tm, KV_W), BF16),
            pltpu.VMEM((m, D_MODEL), BF16),
            pltpu.VMEM((m, D_FF), BF16),
            pltpu.VMEM((N_KV, GROUP * tq, WINDOW + tq), F32)]


def _layer1_prompt(x, sinks, w1, *, tm):
    nb, seq, _ = x.shape
    n_t = seq // tm
    n_tiles = nb * n_t
    cur, prev, cur_stream = _tile_maps(n_t, n_tiles)
    tile = (1, tm, D_MODEL)
    cache = pl.BlockSpec((1, WINDOW, KV_W), cur_stream)
    return pl.pallas_call(
        functools.partial(_layer1_prompt_kernel, tm=tm, n_t=n_t, n_tiles=n_tiles),
        grid=(n_tiles + 1,),
        in_specs=[pl.BlockSpec(tile, cur), pl.BlockSpec(tile, prev), pl.BlockSpec(memory_space=pltpu.SMEM)]
                 + [_resident(a) for a in w1],
        out_specs=[pl.BlockSpec(tile, prev), cache, cache],
        out_shape=[jax.ShapeDtypeStruct((nb, seq, D_MODEL), F32),
                   jax.ShapeDtypeStruct((nb, WINDOW, KV_W), F32),
                   jax.ShapeDtypeStruct((nb, WINDOW, KV_W), F32)],
        scratch_shapes=_layer1_scratch(1, tm, CHUNK),
        compiler_params=_params(1),
        name="layer1_prompt",
    )(x, x, sinks, *w1)


def _layer1_sample(x, kc, vc, sinks, w1):
    ns, tm, _ = x.shape
    whole = lambda shape: pl.BlockSpec(shape, lambda i, _n=len(shape): (0,) * _n)
    cache = whole((ns, WINDOW, KV_W))
    return pl.pallas_call(
        functools.partial(_layer1_sample_kernel, ns=ns, tm=tm),
        grid=(1,),
        in_specs=[whole((ns, tm, D_MODEL)), cache, cache, pl.BlockSpec(memory_space=pltpu.SMEM)]
                 + [_resident(a) for a in w1],
        out_specs=[whole((ns, tm, D_MODEL)), cache, cache],
        out_shape=[jax.ShapeDtypeStruct((ns, tm, D_MODEL), F32),
                   jax.ShapeDtypeStruct((ns, WINDOW, KV_W), F32),
                   jax.ShapeDtypeStruct((ns, WINDOW, KV_W), F32)],
        scratch_shapes=_layer1_scratch(ns, tm, tm) + [pltpu.VMEM((ns * tm, D_MODEL), F32)],
        compiler_params=_params(1),
        name="layer1_sample",
    )(x, kc, vc, sinks, *w1)


def kernel(x_prompt, x_sample, state_conv, cache_k, cache_v, g_mix, g_ffn, w_pw1, b_pw1, w_dw, b_dw,
           ln_g, ln_b, w_pw2, b_pw2, w_qkv, qn_g, kn_g, sinks, w_o, w_gate, w_up, w_down):
    nb = x_prompt.shape[0]
    ndb = x_sample.shape[0]
    row = lambda a: a.reshape(1, -1)
    bf = lambda a: a.astype(BF16)
    w0 = [row(g_mix[0]), bf(w_pw1[0]), row(b_pw1[0]), w_dw[0], row(b_dw[0]), row(ln_g[0]), row(ln_b[0]),
          bf(w_pw2[0]), row(b_pw2[0]), row(g_ffn[0]), bf(w_gate[0]), bf(w_up[0]), bf(w_down[0])]
    w1 = [row(g_mix[1]), bf(w_qkv[0]), row(jnp.tile(qn_g[0], N_HEADS)), row(jnp.tile(kn_g[0], N_KV)),
          bf(w_o[0]), row(g_ffn[1]), bf(w_gate[1]), bf(w_up[1]), bf(w_down[1])]

    zero_state = jnp.zeros((nb, HALO, D_MODEL), F32)
    sample_state = jnp.pad(state_conv[0], ((0, 0), (HALO_PAD, 0), (0, 0)))
    xp1, cst_p = _layer0_prompt(x_prompt, zero_state, w0, tm=PROMPT_TM)
    xs1, cst_s = _layer0_sample(x_sample, sample_state, w0)

    kc = cache_k[0].reshape(ndb, WINDOW, KV_W)
    vc = cache_v[0].reshape(ndb, WINDOW, KV_W)
    yp, kp, vp = _layer1_prompt(xp1, sinks[0], w1, tm=PROMPT_TM)
    ys, ks, vs = _layer1_sample(xs1, kc, vc, sinks[0], w1)

    kv_shape = lambda a: a.reshape(1, a.shape[0], WINDOW, N_KV, HEAD_DIM)
    return (yp, ys, cst_p[None, :, HALO_PAD:, :], cst_s[None, :, HALO_PAD:, :],
            kv_shape(kp), kv_shape(vp), kv_shape(ks), kv_shape(vs))
```

```python
import functools

import jax
import jax.numpy as jnp
from jax import lax
from jax.experimental import pallas as pl
from jax.experimental.pallas import tpu as pltpu

D_MODEL = 1024
CHUNK = 64
CONV_W = 31
HEAD_DIM = 64
N_HEADS = D_MODEL // HEAD_DIM
N_KV = 4
GROUP = N_HEADS // N_KV
WINDOW = 128
D_FF = 2816
KV_W = N_KV * HEAD_DIM
SCALE = HEAD_DIM ** -0.5
NEG = -1e30
RMS_EPS = 1e-6
LN_EPS = 1e-5

SUBLANES = 8
LANES = 128
HALO = -(-(CONV_W - 1) // SUBLANES) * SUBLANES
HALO_PAD = HALO - (CONV_W - 1)
MXU_N = 256
CONV_ROWS = 64
CONV_COLS = 128
LN_ROWS = 64
VMEM_LIMIT = 56 * 1024 * 1024
PROMPT_TM = 256

F32 = jnp.float32
BF16 = jnp.bfloat16


def _dot(a, b):
    return jnp.dot(a, b, preferred_element_type=F32)


def _rms(x, g):
    ms = jnp.mean(x * x, axis=-1, keepdims=True)
    return x * lax.rsqrt(ms + RMS_EPS) * g


def _tok(v):
    return v[:SUBLANES, :LANES]


def _run(gen):
    for _ in gen:
        pass


def _interleave(main, n_main, side, n_side, side_before_last_main=False):
    next(side)
    done = 1
    for i in range(n_main):
        tok = next(main)
        if side_before_last_main:
            want = min(n_side, -(-(n_side * (i + 1)) // (n_main - 1)))
        else:
            want = (n_side * (i + 1)) // n_main
        for _ in range(max(want, 1) - done):
            side.send(tok)
        done = max(want, 1)
    _run(main)
    _run(side)


def _order_after(z, tok):
    if tok is None:
        return z
    bits = pltpu.bitcast(tok, jnp.uint32)
    bits = lax.shift_right_logical(lax.shift_right_logical(bits, jnp.uint32(16)), jnp.uint32(16))
    zero = pltpu.bitcast(bits, F32)
    return z + jnp.concatenate([zero] * (z.shape[0] // zero.shape[0]), axis=0)


FFN_ITEMS = D_FF // MXU_N + D_MODEL // MXU_N


def _ffn_items(x1, gf_ref, wg_ref, wu_ref, wd_ref, act_ref, y_ref):
    hn = _rms(x1, gf_ref[...]).astype(BF16)
    for j in range(D_FF // MXU_N):
        sl = slice(j * MXU_N, (j + 1) * MXU_N)
        gt = _dot(hn, wg_ref[:, sl])
        up = _dot(hn, wu_ref[:, sl])
        act_ref[:, sl] = (gt * jax.nn.sigmoid(gt) * up).astype(BF16)
        yield _tok(gt)
    for j in range(D_MODEL // MXU_N):
        sl = slice(j * MXU_N, (j + 1) * MXU_N)
        dn = _dot(act_ref[...], wd_ref[:, sl])
        y_ref[:, sl] = x1[:, sl] + dn
        yield _tok(dn)


def _l0_mixer_items(x, w, gext_ref, cbuf_ref, sw_ref, *, ns, tm, carry):
    (gm_ref, wpw1_ref, bpw1_ref, wdw_ref, bdw_ref, lng_ref, lnb_ref) = w
    m = ns * tm
    h = _rms(x, gm_ref[...]).astype(BF16)
    a = _dot(h, wpw1_ref[:, :D_MODEL]) + bpw1_ref[:, :D_MODEL]
    gate = _dot(h, wpw1_ref[:, D_MODEL:]) + bpw1_ref[:, D_MODEL:]
    g = a * jax.nn.sigmoid(gate)
    gext_ref[:, HALO:, :] = g.reshape(ns, tm, D_MODEL)
    tok = yield

    rb = min(tm, CONV_ROWS)
    nz = rb + HALO
    for s in range(ns):
        for r0 in range(0, tm, rb):
            for c0 in range(0, D_MODEL, CONV_COLS):
                cs = slice(c0, c0 + CONV_COLS)
                z = _order_after(gext_ref[s, pl.ds(r0, nz), cs], tok)
                acc = None
                for p in range(SUBLANES):
                    rows = rb if p == 0 else rb + SUBLANES
                    part = None
                    for k in range(CONV_W):
                        if (HALO_PAD + k) % SUBLANES != p:
                            continue
                        off = HALO_PAD + k - p
                        term = z[off:off + rows] * wdw_ref[k:k + 1, cs]
                        part = term if part is None else part + term
                    if p:
                        part = pltpu.roll(part, rows - p, axis=0)[:rb]
                    acc = part if acc is None else acc + part
                cbuf_ref[pl.ds(s * tm + r0, rb), cs] = acc + bdw_ref[:, cs]
                tok = yield
    if carry:
        gext_ref[:, :HALO, :] = gext_ref[:, tm:, :]

    lr = min(m, LN_ROWS)
    for r0 in range(0, m, lr):
        c = cbuf_ref[pl.ds(r0, lr), :]
        mu = jnp.mean(c, axis=-1, keepdims=True)
        xc = c - mu
        ln = xc * lax.rsqrt(jnp.mean(xc * xc, axis=-1, keepdims=True) + LN_EPS) * lng_ref[...] + lnb_ref[...]
        sw_ref[pl.ds(r0, lr), :] = (ln * jax.nn.sigmoid(ln)).astype(BF16)
        yield


def _l0_mixer_count(ns, tm):
    rb = min(tm, CONV_ROWS)
    return 1 + ns * (tm // rb) * (D_MODEL // CONV_COLS) + (ns * tm) // min(ns * tm, LN_ROWS)


def _l0_ffn_items(xres, sw_ref, w, act_ref, y_ref):
    (wpw2_ref, bpw2_ref, gf_ref, wg_ref, wu_ref, wd_ref) = w
    x1 = xres + _dot(sw_ref[...], wpw2_ref[...]) + bpw2_ref[...]
    yield _tok(x1)
    yield from _ffn_items(x1, gf_ref, wg_ref, wu_ref, wd_ref, act_ref, y_ref)


def _layer0_prompt_kernel(x_ref, xprev_ref, st_ref, gm_ref, wpw1_ref, bpw1_ref, wdw_ref, bdw_ref,
                          lng_ref, lnb_ref, wpw2_ref, bpw2_ref, gf_ref, wg_ref, wu_ref, wd_ref,
                          y_ref, nst_ref, gext_ref, cbuf_ref, sw_ref, act_ref, *, tm, n_t, n_tiles):
    i = pl.program_id(0)
    t = lax.rem(jnp.minimum(i, n_tiles - 1), n_t)

    @pl.when(i == 0)
    def _():
        sw_ref[...] = jnp.zeros(sw_ref.shape, BF16)

    @pl.when(t == 0)
    def _():
        gext_ref[:, :HALO, :] = st_ref[...]

    ffn = _l0_ffn_items(xprev_ref[0], sw_ref, (wpw2_ref, bpw2_ref, gf_ref, wg_ref, wu_ref, wd_ref),
                        act_ref, y_ref.at[0])
    mixer = _l0_mixer_items(x_ref[0], (gm_ref, wpw1_ref, bpw1_ref, wdw_ref, bdw_ref, lng_ref, lnb_ref),
                            gext_ref, cbuf_ref, sw_ref, ns=1, tm=tm, carry=True)
    _interleave(ffn, 1 + FFN_ITEMS, mixer, _l0_mixer_count(1, tm))

    @pl.when((t == n_t - 1) & (i < n_tiles))
    def _():
        nst_ref[...] = gext_ref[:, tm:, :]


def _layer0_sample_kernel(x_ref, st_ref, gm_ref, wpw1_ref, bpw1_ref, wdw_ref, bdw_ref, lng_ref, lnb_ref,
                          wpw2_ref, bpw2_ref, gf_ref, wg_ref, wu_ref, wd_ref,
                          y_ref, nst_ref, gext_ref, cbuf_ref, sw_ref, act_ref, yflat_ref, *, ns, tm):
    m = ns * tm
    gext_ref[:, :HALO, :] = st_ref[...]
    x = x_ref[...].reshape(m, D_MODEL)
    _run(_l0_mixer_items(x, (gm_ref, wpw1_ref, bpw1_ref, wdw_ref, bdw_ref, lng_ref, lnb_ref),
                         gext_ref, cbuf_ref, sw_ref, ns=ns, tm=tm, carry=False))
    nst_ref[...] = gext_ref[:, tm:, :]
    _run(_l0_ffn_items(x, sw_ref, (wpw2_ref, bpw2_ref, gf_ref, wg_ref, wu_ref, wd_ref), act_ref, yflat_ref))
    y_ref[...] = yflat_ref[...].reshape(ns, tm, D_MODEL)


def _head_mean_square(z):
    r = lax.broadcasted_iota(jnp.int32, (MXU_N, MXU_N), 0) // HEAD_DIM
    c = lax.broadcasted_iota(jnp.int32, (MXU_N, MXU_N), 1) // HEAD_DIM
    ones_bd = jnp.where(r == c, 1.0, 0.0).astype(BF16)
    outs = []
    for j in range(z.shape[1] // MXU_N):
        zz = z[:, j * MXU_N:(j + 1) * MXU_N]
        zz = zz * zz
        hi = zz.astype(BF16)
        lo = (zz - hi.astype(F32)).astype(BF16)
        outs.append(_dot(hi, ones_bd) + _dot(lo, ones_bd))
    ss = outs[0] if len(outs) == 1 else jnp.concatenate(outs, axis=1)
    return ss * (1.0 / HEAD_DIM)


def _init_alibi_bias(bias_ref, tq):
    rows4, nk = bias_ref.shape[1:]
    rows = lax.broadcasted_iota(jnp.int32, (rows4, nk), 0)
    cols = lax.broadcasted_iota(jnp.int32, (rows4, nk), 1)
    dist = jnp.abs((rows % tq) + WINDOW - cols).astype(F32)
    grp = rows // tq
    for kvh in range(N_KV):
        slope = jnp.zeros((rows4, nk), F32)
        for g_ in range(GROUP):
            slope = jnp.where(grp == g_, 2.0 ** (-8.0 * (kvh * GROUP + g_ + 1) / N_HEADS), slope)
        bias_ref[kvh] = -(slope * dist)


def _l1_mixer_items(x, w, caches, sinks_ref, ko_ref, vo_ref, qn_ref, kbuf_ref, vbuf_ref, ao_ref, bias_ref,
                    *, ns, tm, tq, first_pos, carry):
    (gm_ref, wqkv_ref, qg_ref, kg_ref) = w
    nk = WINDOW + tq
    rows4 = GROUP * tq
    h = _rms(x, gm_ref[...]).astype(BF16)
    q = _dot(h, wqkv_ref[:, :D_MODEL])
    k = _dot(h, wqkv_ref[:, D_MODEL:D_MODEL + KV_W])
    v = _dot(h, wqkv_ref[:, D_MODEL + KV_W:])
    qn = q * lax.rsqrt(_head_mean_square(q) + RMS_EPS) * (qg_ref[...] * SCALE)
    kn = k * lax.rsqrt(_head_mean_square(k) + RMS_EPS) * kg_ref[...]
    qn_ref[...] = qn.astype(BF16)
    kbuf_ref[:, WINDOW:, :] = kn.astype(BF16).reshape(ns, tm, KV_W)
    vbuf_ref[:, WINDOW:, :] = v.astype(BF16).reshape(ns, tm, KV_W)
    kn3 = kn.reshape(ns, tm, KV_W)
    v3 = v.reshape(ns, tm, KV_W)
    if tm >= WINDOW:
        ko_ref[...] = kn3[:, tm - WINDOW:, :]
        vo_ref[...] = v3[:, tm - WINDOW:, :]
    else:
        kc_ref, vc_ref = caches
        ko_ref[:, :WINDOW - tm, :] = kc_ref[:, tm:, :]
        vo_ref[:, :WINDOW - tm, :] = vc_ref[:, tm:, :]
        ko_ref[:, WINDOW - tm:, :] = kn3
        vo_ref[:, WINDOW - tm:, :] = v3
    yield

    grp_col = lax.broadcasted_iota(jnp.int32, (rows4, 1), 0) // tq
    sink_cols = []
    for kvh in range(N_KV):
        sc = jnp.zeros((rows4, 1), F32)
        for g_ in range(GROUP):
            sc = jnp.where(grp_col == g_, sinks_ref[kvh * GROUP + g_], sc)
        sink_cols.append(sc)
    key_col = lax.broadcasted_iota(jnp.int32, (rows4, nk), 1)

    for s in range(ns):
        for c in range(tm // tq):
            r0 = c * tq
            row0 = s * tm + r0
            for kvh in range(N_KV):
                q4 = jnp.concatenate(
                    [qn_ref[pl.ds(row0, tq), (kvh * GROUP + g_) * HEAD_DIM:(kvh * GROUP + g_ + 1) * HEAD_DIM]
                     for g_ in range(GROUP)], axis=0)
                ks = slice(kvh * HEAD_DIM, (kvh + 1) * HEAD_DIM)
                kb = kbuf_ref[s, pl.ds(r0, nk), ks]
                vb = vbuf_ref[s, pl.ds(r0, nk), ks]
                sc = lax.dot_general(q4, kb, (((1,), (1,)), ((), ())), preferred_element_type=F32)
                sc = sc + bias_ref[kvh]
                if first_pos is not None:
                    sc = jnp.where(key_col >= WINDOW - r0 - first_pos, sc, NEG)
                sink = sink_cols[kvh]
                mx = jnp.maximum(jnp.max(sc, axis=-1, keepdims=True), sink)
                p = jnp.exp(sc - mx)
                denom = jnp.sum(p, axis=-1, keepdims=True) + jnp.exp(sink - mx)
                o = _dot(p.astype(BF16), vb) / denom
                o_cat = jnp.concatenate([o[g_ * tq:(g_ + 1) * tq] for g_ in range(GROUP)], axis=1)
                ao_ref[pl.ds(row0, tq), kvh * GROUP * HEAD_DIM:(kvh + 1) * GROUP * HEAD_DIM] = o_cat.astype(BF16)
                yield
    if carry:
        kbuf_ref[:, :WINDOW, :] = kbuf_ref[:, tm:, :]
        vbuf_ref[:, :WINDOW, :] = vbuf_ref[:, tm:, :]


def _l1_mixer_count(ns, tm, tq):
    return 1 + ns * (tm // tq) * N_KV


def _l1_ffn_items(xres, ao_ref, w, act_ref, y_ref):
    (wo_ref, gf_ref, wg_ref, wu_ref, wd_ref) = w
    x1 = xres + _dot(ao_ref[...], wo_ref[...])
    yield _tok(x1)
    yield from _ffn_items(x1, gf_ref, wg_ref, wu_ref, wd_ref, act_ref, y_ref)


def _l1_ffn_then_proj_items(x, x1_ref, ao_ref, w, act_ref, y_ref):
    (wo_ref, gf_ref, wg_ref, wu_ref, wd_ref) = w
    yield from _ffn_items(x1_ref[...], gf_ref, wg_ref, wu_ref, wd_ref, act_ref, y_ref)
    x1 = x + _dot(ao_ref[...], wo_ref[...])
    x1_ref[...] = x1
    yield _tok(x1)


def _layer1_prompt_kernel(x_ref, sinks_ref, gm_ref, wqkv_ref, qg_ref, kg_ref, wo_ref, gf_ref,
                          wg_ref, wu_ref, wd_ref, y_ref, ko_ref, vo_ref,
                          qn_ref, kbuf_ref, vbuf_ref, ao_ref, act_ref, bias_ref, x1_ref, *, tm, n_t, n_tiles):
    i = pl.program_id(0)
    t = lax.rem(jnp.minimum(i, n_tiles - 1), n_t)

    @pl.when(i == 0)
    def _():
        _init_alibi_bias(bias_ref, CHUNK)
        x1_ref[...] = jnp.zeros(x1_ref.shape, F32)

    @pl.when(t == 0)
    def _():
        kbuf_ref[:, :WINDOW, :] = jnp.zeros((1, WINDOW, KV_W), BF16)
        vbuf_ref[:, :WINDOW, :] = jnp.zeros((1, WINDOW, KV_W), BF16)

    x = x_ref[0]
    ffn = _l1_ffn_then_proj_items(x, x1_ref, ao_ref, (wo_ref, gf_ref, wg_ref, wu_ref, wd_ref), act_ref, y_ref.at[0])
    mixer = _l1_mixer_items(x, (gm_ref, wqkv_ref, qg_ref, kg_ref), None, sinks_ref, ko_ref, vo_ref,
                            qn_ref, kbuf_ref, vbuf_ref, ao_ref, bias_ref,
                            ns=1, tm=tm, tq=CHUNK, first_pos=t * tm, carry=True)
    _interleave(ffn, FFN_ITEMS + 1, mixer, _l1_mixer_count(1, tm, CHUNK), side_before_last_main=True)


def _layer1_sample_kernel(x_ref, kc_ref, vc_ref, sinks_ref, gm_ref, wqkv_ref, qg_ref, kg_ref, wo_ref, gf_ref,
                          wg_ref, wu_ref, wd_ref, y_ref, ko_ref, vo_ref,
                          qn_ref, kbuf_ref, vbuf_ref, ao_ref, act_ref, bias_ref, yflat_ref, *, ns, tm):
    m = ns * tm
    _init_alibi_bias(bias_ref, tm)
    kbuf_ref[:, :WINDOW, :] = kc_ref[...].astype(BF16)
    vbuf_ref[:, :WINDOW, :] = vc_ref[...].astype(BF16)
    x = x_ref[...].reshape(m, D_MODEL)
    _run(_l1_mixer_items(x, (gm_ref, wqkv_ref, qg_ref, kg_ref), (kc_ref, vc_ref), sinks_ref, ko_ref, vo_ref,
                         qn_ref, kbuf_ref, vbuf_ref, ao_ref, bias_ref,
                         ns=ns, tm=tm, tq=tm, first_pos=None, carry=False))
    _run(_l1_ffn_items(x, ao_ref, (wo_ref, gf_ref, wg_ref, wu_ref, wd_ref), act_ref, yflat_ref))
    y_ref[...] = yflat_ref[...].reshape(ns, tm, D_MODEL)


def _resident(arr, layer=None):
    if layer is None:
        nd = arr.ndim
        return pl.BlockSpec(arr.shape, lambda *_, _nd=nd: (0,) * _nd, pipeline_mode=pl.Buffered(1))
    nd = arr.ndim - 1
    return pl.BlockSpec((None,) + arr.shape[1:], lambda *_, _nd=nd, _l=layer: (_l,) + (0,) * _nd,
                        pipeline_mode=pl.Buffered(1))


def _params():
    return pltpu.CompilerParams(dimension_semantics=("arbitrary",), vmem_limit_bytes=VMEM_LIMIT)


def _tile_maps(n_t, n_tiles):
    def cur(i):
        j = jnp.minimum(i, n_tiles - 1)
        return (j // n_t, j % n_t, 0)

    def prev(i):
        j = jnp.maximum(i - 1, 0)
        return (j // n_t, j % n_t, 0)

    def cur_stream(i):
        return (jnp.minimum(i, n_tiles - 1) // n_t, 0, 0)

    return cur, prev, cur_stream


def _whole(shape):
    return pl.BlockSpec(shape, lambda i, _n=len(shape): (0,) * _n)


def _layer0_prompt(x, state, w0, *, tm):
    nb, seq, _ = x.shape
    n_t = seq // tm
    n_tiles = nb * n_t
    cur, prev, cur_stream = _tile_maps(n_t, n_tiles)
    tile = (1, tm, D_MODEL)
    return pl.pallas_call(
        functools.partial(_layer0_prompt_kernel, tm=tm, n_t=n_t, n_tiles=n_tiles),
        grid=(n_tiles + 1,),
        in_specs=[pl.BlockSpec(tile, cur), pl.BlockSpec(tile, prev), pl.BlockSpec((1, HALO, D_MODEL), cur_stream)]
                 + [_resident(a, l) for a, l in w0],
        out_specs=[pl.BlockSpec(tile, prev), pl.BlockSpec((1, HALO, D_MODEL), cur_stream)],
        out_shape=[jax.ShapeDtypeStruct((nb, seq, D_MODEL), F32),
                   jax.ShapeDtypeStruct((nb, HALO, D_MODEL), F32)],
        scratch_shapes=[pltpu.VMEM((1, HALO + tm, D_MODEL), F32),
                        pltpu.VMEM((tm, D_MODEL), F32),
                        pltpu.VMEM((tm, D_MODEL), BF16),
                        pltpu.VMEM((tm, D_FF), BF16)],
        compiler_params=_params(),
        name="layer0_prompt",
    )(x, x, state, *[a for a, _ in w0])


def _layer0_sample(x, state, w0):
    ns, tm, _ = x.shape
    m = ns * tm
    return pl.pallas_call(
        functools.partial(_layer0_sample_kernel, ns=ns, tm=tm),
        grid=(1,),
        in_specs=[_whole((ns, tm, D_MODEL)), _whole((ns, HALO, D_MODEL))] + [_resident(a, l) for a, l in w0],
        out_specs=[_whole((ns, tm, D_MODEL)), _whole((ns, HALO, D_MODEL))],
        out_shape=[jax.ShapeDtypeStruct((ns, tm, D_MODEL), F32),
                   jax.ShapeDtypeStruct((ns, HALO, D_MODEL), F32)],
        scratch_shapes=[pltpu.VMEM((ns, HALO + tm, D_MODEL), F32),
                        pltpu.VMEM((m, D_MODEL), F32),
                        pltpu.VMEM((m, D_MODEL), BF16),
                        pltpu.VMEM((m, D_FF), BF16),
                        pltpu.VMEM((m, D_MODEL), F32)],
        compiler_params=_params(),
        name="layer0_sample",
    )(x, state, *[a for a, _ in w0])


def _layer1_scratch(ns, tm, tq):
    m = ns * tm
    return [pltpu.VMEM((m, D_MODEL), BF16),
            pltpu.VMEM((ns, WINDOW + tm, KV_W), BF16),
            pltpu.VMEM((ns, WINDOW + tm, KV_W), BF16),
            pltpu.VMEM((m, D_MODEL), BF16),
            pltpu.VMEM((m, D_FF), BF16),
            pltpu.VMEM((N_KV, GROUP * tq, WINDOW + tq), F32),
            pltpu.VMEM((m, D_MODEL), F32)]


def _layer1_prompt(x, sinks, w1, *, tm):
    nb, seq, _ = x.shape
    n_t = seq // tm
    n_tiles = nb * n_t
    cur, prev, cur_stream = _tile_maps(n_t, n_tiles)
    tile = (1, tm, D_MODEL)
    cache = pl.BlockSpec((1, WINDOW, KV_W), cur_stream)
    return pl.pallas_call(
        functools.partial(_layer1_prompt_kernel, tm=tm, n_t=n_t, n_tiles=n_tiles),
        grid=(n_tiles + 1,),
        in_specs=[pl.BlockSpec(tile, cur), pl.BlockSpec(memory_space=pltpu.SMEM)]
                 + [_resident(a, l) for a, l in w1],
        out_specs=[pl.BlockSpec(tile, prev), cache, cache],
        out_shape=[jax.ShapeDtypeStruct((nb, seq, D_MODEL), F32),
                   jax.ShapeDtypeStruct((nb, WINDOW, KV_W), F32),
                   jax.ShapeDtypeStruct((nb, WINDOW, KV_W), F32)],
        scratch_shapes=_layer1_scratch(1, tm, CHUNK),
        compiler_params=_params(),
        name="layer1_prompt",
    )(x, sinks, *[a for a, _ in w1])


def _layer1_sample(x, kc, vc, sinks, w1):
    ns, tm, _ = x.shape
    cache = _whole((ns, WINDOW, KV_W))
    return pl.pallas_call(
        functools.partial(_layer1_sample_kernel, ns=ns, tm=tm),
        grid=(1,),
        in_specs=[_whole((ns, tm, D_MODEL)), cache, cache, pl.BlockSpec(memory_space=pltpu.SMEM)]
                 + [_resident(a, l) for a, l in w1],
        out_specs=[_whole((ns, tm, D_MODEL)), cache, cache],
        out_shape=[jax.ShapeDtypeStruct((ns, tm, D_MODEL), F32),
                   jax.ShapeDtypeStruct((ns, WINDOW, KV_W), F32),
                   jax.ShapeDtypeStruct((ns, WINDOW, KV_W), F32)],
        scratch_shapes=_layer1_scratch(ns, tm, tm),
        compiler_params=_params(),
        name="layer1_sample",
    )(x, kc, vc, sinks, *[a for a, _ in w1])


def kernel(x_prompt, x_sample, state_conv, cache_k, cache_v, g_mix, g_ffn, w_pw1, b_pw1, w_dw, b_dw,
           ln_g, ln_b, w_pw2, b_pw2, w_qkv, qn_g, kn_g, sinks, w_o, w_gate, w_up, w_down):
    nb = x_prompt.shape[0]
    ndb = x_sample.shape[0]
    row = lambda a: (a.reshape(1, -1), None)
    wg, wu, wd = w_gate.astype(BF16), w_up.astype(BF16), w_down.astype(BF16)
    w0 = [row(g_mix[0]), (w_pw1.astype(BF16), 0), row(b_pw1[0]), (w_dw, 0), row(b_dw[0]), row(ln_g[0]),
          row(ln_b[0]), (w_pw2.astype(BF16), 0), row(b_pw2[0]), row(g_ffn[0]), (wg, 0), (wu, 0), (wd, 0)]
    w1 = [row(g_mix[1]), (w_qkv.astype(BF16), 0), row(jnp.tile(qn_g[0], N_HEADS)), row(jnp.tile(kn_g[0], N_KV)),
          (w_o.astype(BF16), 0), row(g_ffn[1]), (wg, 1), (wu, 1), (wd, 1)]

    zero_state = jnp.zeros((nb, HALO, D_MODEL), F32)
    sample_state = jnp.pad(state_conv[0], ((0, 0), (HALO_PAD, 0), (0, 0)))
    xp1, cst_p = _layer0_prompt(x_prompt, zero_state, w0, tm=PROMPT_TM)
    xs1, cst_s = _layer0_sample(x_sample, sample_state, w0)

    kc = cache_k[0].reshape(ndb, WINDOW, KV_W)
    vc = cache_v[0].reshape(ndb, WINDOW, KV_W)
    yp, kp, vp = _layer1_prompt(xp1, sinks[0], w1, tm=PROMPT_TM)
    ys, ks, vs = _layer1_sample(xs1, kc, vc, sinks[0], w1)

    kv_shape = lambda a: a.reshape(1, a.shape[0], WINDOW, N_KV, HEAD_DIM)
    return (yp, ys, cst_p[None, :, HALO_PAD:, :], cst_s[None, :, HALO_PAD:, :],
            kv_shape(kp), kv_shape(vp), kv_shape(ks), kv_shape(vs))
```

```python
import functools

import jax
import jax.numpy as jnp
from jax import lax
from jax.experimental import pallas as pl
from jax.experimental.pallas import tpu as pltpu

D_MODEL = 1024
CHUNK = 64
CONV_W = 31
HEAD_DIM = 64
N_HEADS = D_MODEL // HEAD_DIM
N_KV = 4
GROUP = N_HEADS // N_KV
WINDOW = 128
D_FF = 2816
KV_W = N_KV * HEAD_DIM
SCALE = HEAD_DIM ** -0.5
NEG = -1e30
RMS_EPS = 1e-6
LN_EPS = 1e-5

SUBLANES = 8
LANES = 128
HALO = -(-(CONV_W - 1) // SUBLANES) * SUBLANES
HALO_PAD = HALO - (CONV_W - 1)
MXU_N = 256
CONV_ROWS = 64
CONV_COLS = 128
LN_ROWS = 64
VMEM_LIMIT = 56 * 1024 * 1024
PROMPT_TM = 512

F32 = jnp.float32
BF16 = jnp.bfloat16


def _dot(a, b):
    return jnp.dot(a, b, preferred_element_type=F32)


def _rms(x, g):
    ms = jnp.mean(x * x, axis=-1, keepdims=True)
    return x * lax.rsqrt(ms + RMS_EPS) * g


def _tok(v):
    return v[:SUBLANES, :LANES]


def _run(gen):
    for _ in gen:
        pass


def _interleave(main, n_main, side, n_side, side_before_last_main=False):
    next(side)
    done = 1
    for i in range(n_main):
        tok = next(main)
        if side_before_last_main:
            want = min(n_side, -(-(n_side * (i + 1)) // (n_main - 1)))
        else:
            want = (n_side * (i + 1)) // n_main
        for _ in range(max(want, 1) - done):
            side.send(tok)
        done = max(want, 1)
    _run(main)
    _run(side)


def _order_after(z, tok):
    if tok is None:
        return z
    bits = pltpu.bitcast(tok, jnp.uint32)
    bits = lax.shift_right_logical(lax.shift_right_logical(bits, jnp.uint32(16)), jnp.uint32(16))
    zero = pltpu.bitcast(bits, F32)
    return z + jnp.concatenate([zero] * (z.shape[0] // zero.shape[0]), axis=0)


FFN_ITEMS = D_FF // MXU_N + D_MODEL // MXU_N


def _ffn_items(x1, gf_ref, wg_ref, wu_ref, wd_ref, act_ref, y_ref):
    hn = _rms(x1, gf_ref[...]).astype(BF16)
    for j in range(D_FF // MXU_N):
        sl = slice(j * MXU_N, (j + 1) * MXU_N)
        gt = _dot(hn, wg_ref[:, sl])
        up = _dot(hn, wu_ref[:, sl])
        act_ref[:, sl] = (gt * jax.nn.sigmoid(gt) * up).astype(BF16)
        yield _tok(gt)
    for j in range(D_MODEL // MXU_N):
        sl = slice(j * MXU_N, (j + 1) * MXU_N)
        dn = _dot(act_ref[...], wd_ref[:, sl])
        y_ref[:, sl] = x1[:, sl] + dn
        yield _tok(dn)


def _l0_mixer_items(x, w, gext_ref, cbuf_ref, sw_ref, *, ns, tm, carry):
    (gm_ref, wpw1_ref, bpw1_ref, wdw_ref, bdw_ref, lng_ref, lnb_ref) = w
    m = ns * tm
    h = _rms(x, gm_ref[...]).astype(BF16)
    a = _dot(h, wpw1_ref[:, :D_MODEL]) + bpw1_ref[:, :D_MODEL]
    gate = _dot(h, wpw1_ref[:, D_MODEL:]) + bpw1_ref[:, D_MODEL:]
    g = a * jax.nn.sigmoid(gate)
    gext_ref[:, HALO:, :] = g.reshape(ns, tm, D_MODEL)
    tok = yield

    rb = min(tm, CONV_ROWS)
    nz = rb + HALO
    for s in range(ns):
        for r0 in range(0, tm, rb):
            for c0 in range(0, D_MODEL, CONV_COLS):
                cs = slice(c0, c0 + CONV_COLS)
                z = _order_after(gext_ref[s, pl.ds(r0, nz), cs], tok)
                acc = None
                for p in range(SUBLANES):
                    rows = rb if p == 0 else rb + SUBLANES
                    part = None
                    for k in range(CONV_W):
                        if (HALO_PAD + k) % SUBLANES != p:
                            continue
                        off = HALO_PAD + k - p
                        term = z[off:off + rows] * wdw_ref[k:k + 1, cs]
                        part = term if part is None else part + term
                    if p:
                        part = pltpu.roll(part, rows - p, axis=0)[:rb]
                    acc = part if acc is None else acc + part
                cbuf_ref[pl.ds(s * tm + r0, rb), cs] = acc + bdw_ref[:, cs]
                tok = yield
    if carry:
        gext_ref[:, :HALO, :] = gext_ref[:, tm:, :]

    lr = min(m, LN_ROWS)
    for r0 in range(0, m, lr):
        c = cbuf_ref[pl.ds(r0, lr), :]
        mu = jnp.mean(c, axis=-1, keepdims=True)
        xc = c - mu
        ln = xc * lax.rsqrt(jnp.mean(xc * xc, axis=-1, keepdims=True) + LN_EPS) * lng_ref[...] + lnb_ref[...]
        sw_ref[pl.ds(r0, lr), :] = (ln * jax.nn.sigmoid(ln)).astype(BF16)
        yield


def _l0_mixer_count(ns, tm):
    rb = min(tm, CONV_ROWS)
    return 1 + ns * (tm // rb) * (D_MODEL // CONV_COLS) + (ns * tm) // min(ns * tm, LN_ROWS)


def _l0_ffn_items(xres, sw_ref, w, act_ref, y_ref):
    (wpw2_ref, bpw2_ref, gf_ref, wg_ref, wu_ref, wd_ref) = w
    x1 = xres + _dot(sw_ref[...], wpw2_ref[...]) + bpw2_ref[...]
    yield _tok(x1)
    yield from _ffn_items(x1, gf_ref, wg_ref, wu_ref, wd_ref, act_ref, y_ref)


def _layer0_prompt_kernel(x_ref, xprev_ref, st_ref, gm_ref, wpw1_ref, bpw1_ref, wdw_ref, bdw_ref,
                          lng_ref, lnb_ref, wpw2_ref, bpw2_ref, gf_ref, wg_ref, wu_ref, wd_ref,
                          y_ref, nst_ref, gext_ref, cbuf_ref, sw_ref, act_ref, *, tm, n_t, n_tiles):
    i = pl.program_id(0)
    t = lax.rem(jnp.minimum(i, n_tiles - 1), n_t)

    @pl.when(i == 0)
    def _():
        sw_ref[...] = jnp.zeros(sw_ref.shape, BF16)

    @pl.when(t == 0)
    def _():
        gext_ref[:, :HALO, :] = st_ref[...]

    ffn = _l0_ffn_items(xprev_ref[0], sw_ref, (wpw2_ref, bpw2_ref, gf_ref, wg_ref, wu_ref, wd_ref),
                        act_ref, y_ref.at[0])
    mixer = _l0_mixer_items(x_ref[0], (gm_ref, wpw1_ref, bpw1_ref, wdw_ref, bdw_ref, lng_ref, lnb_ref),
                            gext_ref, cbuf_ref, sw_ref, ns=1, tm=tm, carry=True)
    _interleave(ffn, 1 + FFN_ITEMS, mixer, _l0_mixer_count(1, tm))

    @pl.when((t == n_t - 1) & (i < n_tiles))
    def _():
        nst_ref[...] = gext_ref[:, tm:, :]


def _layer0_sample_kernel(x_ref, st_ref, gm_ref, wpw1_ref, bpw1_ref, wdw_ref, bdw_ref, lng_ref, lnb_ref,
                          wpw2_ref, bpw2_ref, gf_ref, wg_ref, wu_ref, wd_ref,
                          y_ref, nst_ref, gext_ref, cbuf_ref, sw_ref, act_ref, yflat_ref, *, ns, tm):
    m = ns * tm
    gext_ref[:, :HALO, :] = st_ref[...]
    x = x_ref[...].reshape(m, D_MODEL)
    _run(_l0_mixer_items(x, (gm_ref, wpw1_ref, bpw1_ref, wdw_ref, bdw_ref, lng_ref, lnb_ref),
                         gext_ref, cbuf_ref, sw_ref, ns=ns, tm=tm, carry=False))
    nst_ref[...] = gext_ref[:, tm:, :]
    _run(_l0_ffn_items(x, sw_ref, (wpw2_ref, bpw2_ref, gf_ref, wg_ref, wu_ref, wd_ref), act_ref, yflat_ref))
    y_ref[...] = yflat_ref[...].reshape(ns, tm, D_MODEL)


def _head_mean_square(z):
    r = lax.broadcasted_iota(jnp.int32, (MXU_N, MXU_N), 0) // HEAD_DIM
    c = lax.broadcasted_iota(jnp.int32, (MXU_N, MXU_N), 1) // HEAD_DIM
    ones_bd = jnp.where(r == c, 1.0, 0.0).astype(BF16)
    outs = []
    for j in range(z.shape[1] // MXU_N):
        zz = z[:, j * MXU_N:(j + 1) * MXU_N]
        zz = zz * zz
        hi = zz.astype(BF16)
        lo = (zz - hi.astype(F32)).astype(BF16)
        outs.append(_dot(hi, ones_bd) + _dot(lo, ones_bd))
    ss = outs[0] if len(outs) == 1 else jnp.concatenate(outs, axis=1)
    return ss * (1.0 / HEAD_DIM)


def _init_alibi_bias(bias_ref, tq):
    rows4, nk = bias_ref.shape[1:]
    rows = lax.broadcasted_iota(jnp.int32, (rows4, nk), 0)
    cols = lax.broadcasted_iota(jnp.int32, (rows4, nk), 1)
    dist = jnp.abs((rows % tq) + WINDOW - cols).astype(F32)
    grp = rows // tq
    for kvh in range(N_KV):
        slope = jnp.zeros((rows4, nk), F32)
        for g_ in range(GROUP):
            slope = jnp.where(grp == g_, 2.0 ** (-8.0 * (kvh * GROUP + g_ + 1) / N_HEADS), slope)
        bias_ref[kvh] = -(slope * dist)


def _l1_mixer_items(x, w, caches, sinks_ref, ko_ref, vo_ref, qn_ref, kbuf_ref, vbuf_ref, ao_ref, bias_ref,
                    *, ns, tm, tq, first_pos, carry):
    (gm_ref, wqkv_ref, qg_ref, kg_ref) = w
    nk = WINDOW + tq
    rows4 = GROUP * tq
    h = _rms(x, gm_ref[...]).astype(BF16)
    q = _dot(h, wqkv_ref[:, :D_MODEL])
    k = _dot(h, wqkv_ref[:, D_MODEL:D_MODEL + KV_W])
    v = _dot(h, wqkv_ref[:, D_MODEL + KV_W:])
    qn = q * lax.rsqrt(_head_mean_square(q) + RMS_EPS) * (qg_ref[...] * SCALE)
    kn = k * lax.rsqrt(_head_mean_square(k) + RMS_EPS) * kg_ref[...]
    qn_ref[...] = qn.astype(BF16)
    kbuf_ref[:, WINDOW:, :] = kn.astype(BF16).reshape(ns, tm, KV_W)
    vbuf_ref[:, WINDOW:, :] = v.astype(BF16).reshape(ns, tm, KV_W)
    kn3 = kn.reshape(ns, tm, KV_W)
    v3 = v.reshape(ns, tm, KV_W)
    if tm >= WINDOW:
        ko_ref[...] = kn3[:, tm - WINDOW:, :]
        vo_ref[...] = v3[:, tm - WINDOW:, :]
    else:
        kc_ref, vc_ref = caches
        ko_ref[:, :WINDOW - tm, :] = kc_ref[:, tm:, :]
        vo_ref[:, :WINDOW - tm, :] = vc_ref[:, tm:, :]
        ko_ref[:, WINDOW - tm:, :] = kn3
        vo_ref[:, WINDOW - tm:, :] = v3
    yield

    grp_col = lax.broadcasted_iota(jnp.int32, (rows4, 1), 0) // tq
    sink_cols = []
    for kvh in range(N_KV):
        sc = jnp.zeros((rows4, 1), F32)
        for g_ in range(GROUP):
            sc = jnp.where(grp_col == g_, sinks_ref[kvh * GROUP + g_], sc)
        sink_cols.append(sc)
    key_col = lax.broadcasted_iota(jnp.int32, (rows4, nk), 1)

    for s in range(ns):
        for c in range(tm // tq):
            r0 = c * tq
            row0 = s * tm + r0
            for kvh in range(N_KV):
                q4 = jnp.concatenate(
                    [qn_ref[pl.ds(row0, tq), (kvh * GROUP + g_) * HEAD_DIM:(kvh * GROUP + g_ + 1) * HEAD_DIM]
                     for g_ in range(GROUP)], axis=0)
                ks = slice(kvh * HEAD_DIM, (kvh + 1) * HEAD_DIM)
                kb = kbuf_ref[s, pl.ds(r0, nk), ks]
                vb = vbuf_ref[s, pl.ds(r0, nk), ks]
                sc = lax.dot_general(q4, kb, (((1,), (1,)), ((), ())), preferred_element_type=F32)
                sc = sc + bias_ref[kvh]
                if first_pos is not None:
                    sc = jnp.where(key_col >= WINDOW - r0 - first_pos, sc, NEG)
                sink = sink_cols[kvh]
                mx = jnp.maximum(jnp.max(sc, axis=-1, keepdims=True), sink)
                p = jnp.exp(sc - mx)
                denom = jnp.sum(p, axis=-1, keepdims=True) + jnp.exp(sink - mx)
                o = _dot(p.astype(BF16), vb) / denom
                o_cat = jnp.concatenate([o[g_ * tq:(g_ + 1) * tq] for g_ in range(GROUP)], axis=1)
                ao_ref[pl.ds(row0, tq), kvh * GROUP * HEAD_DIM:(kvh + 1) * GROUP * HEAD_DIM] = o_cat.astype(BF16)
                yield
    if carry:
        kbuf_ref[:, :WINDOW, :] = kbuf_ref[:, tm:, :]
        vbuf_ref[:, :WINDOW, :] = vbuf_ref[:, tm:, :]


def _l1_mixer_count(ns, tm, tq):
    return 1 + ns * (tm // tq) * N_KV


def _l1_ffn_items(xres, ao_ref, w, act_ref, y_ref):
    (wo_ref, gf_ref, wg_ref, wu_ref, wd_ref) = w
    x1 = xres + _dot(ao_ref[...], wo_ref[...])
    yield _tok(x1)
    yield from _ffn_items(x1, gf_ref, wg_ref, wu_ref, wd_ref, act_ref, y_ref)


def _l1_ffn_then_proj_items(x, x1_ref, ao_ref, w, act_ref, y_ref):
    (wo_ref, gf_ref, wg_ref, wu_ref, wd_ref) = w
    yield from _ffn_items(x1_ref[...], gf_ref, wg_ref, wu_ref, wd_ref, act_ref, y_ref)
    x1 = x + _dot(ao_ref[...], wo_ref[...])
    x1_ref[...] = x1
    yield _tok(x1)


def _layer1_prompt_kernel(x_ref, sinks_ref, gm_ref, wqkv_ref, qg_ref, kg_ref, wo_ref, gf_ref,
                          wg_ref, wu_ref, wd_ref, y_ref, ko_ref, vo_ref,
                          qn_ref, kbuf_ref, vbuf_ref, ao_ref, act_ref, bias_ref, x1_ref, *, tm, n_t, n_tiles):
    i = pl.program_id(0)
    t = lax.rem(jnp.minimum(i, n_tiles - 1), n_t)

    @pl.when(i == 0)
    def _():
        _init_alibi_bias(bias_ref, CHUNK)
        x1_ref[...] = jnp.zeros(x1_ref.shape, F32)

    @pl.when(t == 0)
    def _():
        kbuf_ref[:, :WINDOW, :] = jnp.zeros((1, WINDOW, KV_W), BF16)
        vbuf_ref[:, :WINDOW, :] = jnp.zeros((1, WINDOW, KV_W), BF16)

    x = x_ref[0]
    ffn = _l1_ffn_then_proj_items(x, x1_ref, ao_ref, (wo_ref, gf_ref, wg_ref, wu_ref, wd_ref), act_ref, y_ref.at[0])
    mixer = _l1_mixer_items(x, (gm_ref, wqkv_ref, qg_ref, kg_ref), None, sinks_ref, ko_ref, vo_ref,
                            qn_ref, kbuf_ref, vbuf_ref, ao_ref, bias_ref,
                            ns=1, tm=tm, tq=CHUNK, first_pos=t * tm, carry=True)
    _interleave(ffn, FFN_ITEMS + 1, mixer, _l1_mixer_count(1, tm, CHUNK), side_before_last_main=True)


def _layer1_sample_kernel(x_ref, kc_ref, vc_ref, sinks_ref, gm_ref, wqkv_ref, qg_ref, kg_ref, wo_ref, gf_ref,
                          wg_ref, wu_ref, wd_ref, y_ref, ko_ref, vo_ref,
                          qn_ref, kbuf_ref, vbuf_ref, ao_ref, act_ref, bias_ref, yflat_ref, *, ns, tm):
    m = ns * tm
    _init_alibi_bias(bias_ref, tm)
    kbuf_ref[:, :WINDOW, :] = kc_ref[...].astype(BF16)
    vbuf_ref[:, :WINDOW, :] = vc_ref[...].astype(BF16)
    x = x_ref[...].reshape(m, D_MODEL)
    _run(_l1_mixer_items(x, (gm_ref, wqkv_ref, qg_ref, kg_ref), (kc_ref, vc_ref), sinks_ref, ko_ref, vo_ref,
                         qn_ref, kbuf_ref, vbuf_ref, ao_ref, bias_ref,
                         ns=ns, tm=tm, tq=tm, first_pos=None, carry=False))
    _run(_l1_ffn_items(x, ao_ref, (wo_ref, gf_ref, wg_ref, wu_ref, wd_ref), act_ref, yflat_ref))
    y_ref[...] = yflat_ref[...].reshape(ns, tm, D_MODEL)


def _resident(arr, layer=None):
    if layer is None:
        nd = arr.ndim
        return pl.BlockSpec(arr.shape, lambda *_, _nd=nd: (0,) * _nd, pipeline_mode=pl.Buffered(1))
    nd = arr.ndim - 1
    return pl.BlockSpec((None,) + arr.shape[1:], lambda *_, _nd=nd, _l=layer: (_l,) + (0,) * _nd,
                        pipeline_mode=pl.Buffered(1))


def _params():
    return pltpu.CompilerParams(dimension_semantics=("arbitrary",), vmem_limit_bytes=VMEM_LIMIT)


def _tile_maps(n_t, n_tiles):
    def cur(i):
        j = jnp.minimum(i, n_tiles - 1)
        return (j // n_t, j % n_t, 0)

    def prev(i):
        j = jnp.maximum(i - 1, 0)
        return (j // n_t, j % n_t, 0)

    def cur_stream(i):
        return (jnp.minimum(i, n_tiles - 1) // n_t, 0, 0)

    return cur, prev, cur_stream


def _whole(shape):
    return pl.BlockSpec(shape, lambda i, _n=len(shape): (0,) * _n)


def _layer0_prompt(x, state, w0, *, tm):
    nb, seq, _ = x.shape
    n_t = seq // tm
    n_tiles = nb * n_t
    cur, prev, cur_stream = _tile_maps(n_t, n_tiles)
    tile = (1, tm, D_MODEL)
    return pl.pallas_call(
        functools.partial(_layer0_prompt_kernel, tm=tm, n_t=n_t, n_tiles=n_tiles),
        grid=(n_tiles + 1,),
        in_specs=[pl.BlockSpec(tile, cur), pl.BlockSpec(tile, prev), pl.BlockSpec((1, HALO, D_MODEL), cur_stream)]
                 + [_resident(a, l) for a, l in w0],
        out_specs=[pl.BlockSpec(tile, prev), pl.BlockSpec((1, HALO, D_MODEL), cur_stream)],
        out_shape=[jax.ShapeDtypeStruct((nb, seq, D_MODEL), F32),
                   jax.ShapeDtypeStruct((nb, HALO, D_MODEL), F32)],
        scratch_shapes=[pltpu.VMEM((1, HALO + tm, D_MODEL), F32),
                        pltpu.VMEM((tm, D_MODEL), F32),
                        pltpu.VMEM((tm, D_MODEL), BF16),
                        pltpu.VMEM((tm, D_FF), BF16)],
        compiler_params=_params(),
        name="layer0_prompt",
    )(x, x, state, *[a for a, _ in w0])


def _layer0_sample(x, state, w0):
    ns, tm, _ = x.shape
    m = ns * tm
    return pl.pallas_call(
        functools.partial(_layer0_sample_kernel, ns=ns, tm=tm),
        grid=(1,),
        in_specs=[_whole((ns, tm, D_MODEL)), _whole((ns, HALO, D_MODEL))] + [_resident(a, l) for a, l in w0],
        out_specs=[_whole((ns, tm, D_MODEL)), _whole((ns, HALO, D_MODEL))],
        out_shape=[jax.ShapeDtypeStruct((ns, tm, D_MODEL), F32),
                   jax.ShapeDtypeStruct((ns, HALO, D_MODEL), F32)],
        scratch_shapes=[pltpu.VMEM((ns, HALO + tm, D_MODEL), F32),
                        pltpu.VMEM((m, D_MODEL), F32),
                        pltpu.VMEM((m, D_MODEL), BF16),
                        pltpu.VMEM((m, D_FF), BF16),
                        pltpu.VMEM((m, D_MODEL), F32)],
        compiler_params=_params(),
        name="layer0_sample",
    )(x, state, *[a for a, _ in w0])


def _layer1_scratch(ns, tm, tq):
    m = ns * tm
    return [pltpu.VMEM((m, D_MODEL), BF16),
            pltpu.VMEM((ns, WINDOW + tm, KV_W), BF16),
            pltpu.VMEM((ns, WINDOW + tm, KV_W), BF16),
            pltpu.VMEM((m, D_MODEL), BF16),
            pltpu.VMEM((m, D_FF), BF16),
            pltpu.VMEM((N_KV, GROUP * tq, WINDOW + tq), F32),
            pltpu.VMEM((m, D_MODEL), F32)]


def _layer1_prompt(x, sinks, w1, *, tm):
    nb, seq, _ = x.shape
    n_t = seq // tm
    n_tiles = nb * n_t
    cur, prev, cur_stream = _tile_maps(n_t, n_tiles)
    tile = (1, tm, D_MODEL)
    cache = pl.BlockSpec((1, WINDOW, KV_W), cur_stream)
    return pl.pallas_call(
        functools.partial(_layer1_prompt_kernel, tm=tm, n_t=n_t, n_tiles=n_tiles),
        grid=(n_tiles + 1,),
        in_specs=[pl.BlockSpec(tile, cur), pl.BlockSpec(memory_space=pltpu.SMEM)]
                 + [_resident(a, l) for a, l in w1],
        out_specs=[pl.BlockSpec(tile, prev), cache, cache],
        out_shape=[jax.ShapeDtypeStruct((nb, seq, D_MODEL), F32),
                   jax.ShapeDtypeStruct((nb, WINDOW, KV_W), F32),
                   jax.ShapeDtypeStruct((nb, WINDOW, KV_W), F32)],
        scratch_shapes=_layer1_scratch(1, tm, CHUNK),
        compiler_params=_params(),
        name="layer1_prompt",
    )(x, sinks, *[a for a, _ in w1])


def _layer1_sample(x, kc, vc, sinks, w1):
    ns, tm, _ = x.shape
    cache = _whole((ns, WINDOW, KV_W))
    return pl.pallas_call(
        functools.partial(_layer1_sample_kernel, ns=ns, tm=tm),
        grid=(1,),
        in_specs=[_whole((ns, tm, D_MODEL)), cache, cache, pl.BlockSpec(memory_space=pltpu.SMEM)]
                 + [_resident(a, l) for a, l in w1],
        out_specs=[_whole((ns, tm, D_MODEL)), cache, cache],
        out_shape=[jax.ShapeDtypeStruct((ns, tm, D_MODEL), F32),
                   jax.ShapeDtypeStruct((ns, WINDOW, KV_W), F32),
                   jax.ShapeDtypeStruct((ns, WINDOW, KV_W), F32)],
        scratch_shapes=_layer1_scratch(ns, tm, tm),
        compiler_params=_params(),
        name="layer1_sample",
    )(x, kc, vc, sinks, *[a for a, _ in w1])


def kernel(x_prompt, x_sample, state_conv, cache_k, cache_v, g_mix, g_ffn, w_pw1, b_pw1, w_dw, b_dw,
           ln_g, ln_b, w_pw2, b_pw2, w_qkv, qn_g, kn_g, sinks, w_o, w_gate, w_up, w_down):
    nb = x_prompt.shape[0]
    ndb = x_sample.shape[0]
    row = lambda a: (a.reshape(1, -1), None)
    wg, wu, wd = w_gate.astype(BF16), w_up.astype(BF16), w_down.astype(BF16)
    w0 = [row(g_mix[0]), (w_pw1.astype(BF16), 0), row(b_pw1[0]), (w_dw, 0), row(b_dw[0]), row(ln_g[0]),
          row(ln_b[0]), (w_pw2.astype(BF16), 0), row(b_pw2[0]), row(g_ffn[0]), (wg, 0), (wu, 0), (wd, 0)]
    w1 = [row(g_mix[1]), (w_qkv.astype(BF16), 0), row(jnp.tile(qn_g[0], N_HEADS)), row(jnp.tile(kn_g[0], N_KV)),
          (w_o.astype(BF16), 0), row(g_ffn[1]), (wg, 1), (wu, 1), (wd, 1)]

    zero_state = jnp.zeros((nb, HALO, D_MODEL), F32)
    sample_state = jnp.pad(state_conv[0], ((0, 0), (HALO_PAD, 0), (0, 0)))
    xp1, cst_p = _layer0_prompt(x_prompt, zero_state, w0, tm=PROMPT_TM)
    xs1, cst_s = _layer0_sample(x_sample, sample_state, w0)

    kc = cache_k[0].reshape(ndb, WINDOW, KV_W)
    vc = cache_v[0].reshape(ndb, WINDOW, KV_W)
    yp, kp, vp = _layer1_prompt(xp1, sinks[0], w1, tm=PROMPT_TM)
    ys, ks, vs = _layer1_sample(xs1, kc, vc, sinks[0], w1)

    kv_shape = lambda a: a.reshape(1, a.shape[0], WINDOW, N_KV, HEAD_DIM)
    return (yp, ys, cst_p[None, :, HALO_PAD:, :], cst_s[None, :, HALO_PAD:, :],
            kv_shape(kp), kv_shape(vp), kv_shape(ks), kv_shape(vs))
```

```python
import functools

import jax
import jax.numpy as jnp
from jax import lax
from jax.experimental import pallas as pl
from jax.experimental.pallas import tpu as pltpu

D_MODEL = 1024
CHUNK = 64
CONV_W = 31
HEAD_DIM = 64
N_HEADS = D_MODEL // HEAD_DIM
N_KV = 4
GROUP = N_HEADS // N_KV
WINDOW = 128
D_FF = 2816
KV_W = N_KV * HEAD_DIM
SCALE = HEAD_DIM ** -0.5
NEG = -1e30
RMS_EPS = 1e-6
LN_EPS = 1e-5

SUBLANES = 8
LANES = 128
HALO = -(-(CONV_W - 1) // SUBLANES) * SUBLANES
HALO_PAD = HALO - (CONV_W - 1)
MXU_N = 256
CONV_ROWS = 64
CONV_COLS = 128
LN_ROWS = 64
VMEM_LIMIT = 56 * 1024 * 1024
PROMPT_TM = 512

F32 = jnp.float32
BF16 = jnp.bfloat16


def _dot(a, b):
    return jnp.dot(a, b, preferred_element_type=F32)


def _rms(x, g):
    ms = jnp.mean(x * x, axis=-1, keepdims=True)
    return x * lax.rsqrt(ms + RMS_EPS) * g


def _tok(v):
    return v[:SUBLANES, :LANES]


def _run(gen):
    for _ in gen:
        pass


def _interleave(main, n_main, side, n_side, side_before_last_main=False):
    next(side)
    done = 1
    for i in range(n_main):
        tok = next(main)
        if side_before_last_main:
            want = min(n_side, -(-(n_side * (i + 1)) // (n_main - 1)))
        else:
            want = (n_side * (i + 1)) // n_main
        for _ in range(max(want, 1) - done):
            side.send(tok)
        done = max(want, 1)
    _run(main)
    _run(side)


def _order_after(z, tok):
    if tok is None:
        return z
    bits = pltpu.bitcast(tok, jnp.uint32)
    bits = lax.shift_right_logical(lax.shift_right_logical(bits, jnp.uint32(16)), jnp.uint32(16))
    zero = pltpu.bitcast(bits, F32)
    return z + jnp.concatenate([zero] * (z.shape[0] // zero.shape[0]), axis=0)


FFN_ITEMS = D_FF // MXU_N + D_MODEL // MXU_N


def _ffn_items(x1, gf_ref, wg_ref, wu_ref, wd_ref, act_ref, y_ref):
    hn = _rms(x1, gf_ref[...]).astype(BF16)
    for j in range(D_FF // MXU_N):
        sl = slice(j * MXU_N, (j + 1) * MXU_N)
        gt = _dot(hn, wg_ref[:, sl])
        up = _dot(hn, wu_ref[:, sl])
        act_ref[:, sl] = (gt * jax.nn.sigmoid(gt) * up).astype(BF16)
        yield _tok(gt)
    for j in range(D_MODEL // MXU_N):
        sl = slice(j * MXU_N, (j + 1) * MXU_N)
        dn = _dot(act_ref[...], wd_ref[:, sl])
        y_ref[:, sl] = x1[:, sl] + dn
        yield _tok(dn)


def _l0_mixer_items(x, w, gext_ref, cbuf_ref, sw_ref, *, ns, tm, carry):
    (gm_ref, wpw1_ref, bpw1_ref, wdw_ref, bdw_ref, lng_ref, lnb_ref) = w
    m = ns * tm
    h = _rms(x, gm_ref[...]).astype(BF16)
    a = _dot(h, wpw1_ref[:, :D_MODEL]) + bpw1_ref[:, :D_MODEL]
    gate = _dot(h, wpw1_ref[:, D_MODEL:]) + bpw1_ref[:, D_MODEL:]
    g = a * jax.nn.sigmoid(gate)
    gext_ref[:, HALO:, :] = g.reshape(ns, tm, D_MODEL)
    tok = yield

    rb = min(tm, CONV_ROWS)
    nz = rb + HALO
    for s in range(ns):
        for r0 in range(0, tm, rb):
            for c0 in range(0, D_MODEL, CONV_COLS):
                cs = slice(c0, c0 + CONV_COLS)
                z = _order_after(gext_ref[s, pl.ds(r0, nz), cs], tok)
                acc = None
                for p in range(SUBLANES):
                    rows = rb if p == 0 else rb + SUBLANES
                    part = None
                    for k in range(CONV_W):
                        if (HALO_PAD + k) % SUBLANES != p:
                            continue
                        off = HALO_PAD + k - p
                        term = z[off:off + rows] * wdw_ref[k:k + 1, cs]
                        part = term if part is None else part + term
                    if p:
                        part = pltpu.roll(part, rows - p, axis=0)[:rb]
                    acc = part if acc is None else acc + part
                cbuf_ref[pl.ds(s * tm + r0, rb), cs] = acc + bdw_ref[:, cs]
                tok = yield
    if carry:
        gext_ref[:, :HALO, :] = gext_ref[:, tm:, :]

    lr = min(m, LN_ROWS)
    for r0 in range(0, m, lr):
        c = cbuf_ref[pl.ds(r0, lr), :]
        mu = jnp.mean(c, axis=-1, keepdims=True)
        xc = c - mu
        ln = xc * lax.rsqrt(jnp.mean(xc * xc, axis=-1, keepdims=True) + LN_EPS) * lng_ref[...] + lnb_ref[...]
        sw_ref[pl.ds(r0, lr), :] = (ln * jax.nn.sigmoid(ln)).astype(BF16)
        yield


def _l0_mixer_count(ns, tm):
    rb = min(tm, CONV_ROWS)
    return 1 + ns * (tm // rb) * (D_MODEL // CONV_COLS) + (ns * tm) // min(ns * tm, LN_ROWS)


def _l0_ffn_items(xres, sw_ref, w, act_ref, y_ref):
    (wpw2_ref, bpw2_ref, gf_ref, wg_ref, wu_ref, wd_ref) = w
    x1 = xres + _dot(sw_ref[...], wpw2_ref[...]) + bpw2_ref[...]
    yield _tok(x1)
    yield from _ffn_items(x1, gf_ref, wg_ref, wu_ref, wd_ref, act_ref, y_ref)


def _layer0_prompt_kernel(x_ref, xprev_ref, st_ref, gm_ref, wpw1_ref, bpw1_ref, wdw_ref, bdw_ref,
                          lng_ref, lnb_ref, wpw2_ref, bpw2_ref, gf_ref, wg_ref, wu_ref, wd_ref,
                          y_ref, nst_ref, gext_ref, cbuf_ref, sw_ref, act_ref, *, tm, n_t, n_tiles):
    i = pl.program_id(0)
    t = lax.rem(jnp.minimum(i, n_tiles - 1), n_t)

    @pl.when(t == 0)
    def _():
        gext_ref[:, :HALO, :] = st_ref[...]

    def ffn():
        return _l0_ffn_items(xprev_ref[0], sw_ref, (wpw2_ref, bpw2_ref, gf_ref, wg_ref, wu_ref, wd_ref),
                             act_ref, y_ref.at[0])

    def mixer():
        return _l0_mixer_items(x_ref[0], (gm_ref, wpw1_ref, bpw1_ref, wdw_ref, bdw_ref, lng_ref, lnb_ref),
                               gext_ref, cbuf_ref, sw_ref, ns=1, tm=tm, carry=True)

    @pl.when(i == 0)
    def _():
        _run(mixer())

    @pl.when((i > 0) & (i < n_tiles))
    def _():
        _interleave(ffn(), 1 + FFN_ITEMS, mixer(), _l0_mixer_count(1, tm))

    @pl.when(i == n_tiles)
    def _():
        _run(ffn())

    @pl.when((t == n_t - 1) & (i < n_tiles))
    def _():
        nst_ref[...] = gext_ref[:, tm:, :]


def _layer0_sample_kernel(x_ref, st_ref, gm_ref, wpw1_ref, bpw1_ref, wdw_ref, bdw_ref, lng_ref, lnb_ref,
                          wpw2_ref, bpw2_ref, gf_ref, wg_ref, wu_ref, wd_ref,
                          y_ref, nst_ref, gext_ref, cbuf_ref, sw_ref, act_ref, yflat_ref, *, ns, tm):
    m = ns * tm
    gext_ref[:, :HALO, :] = st_ref[...]
    x = x_ref[...].reshape(m, D_MODEL)
    _run(_l0_mixer_items(x, (gm_ref, wpw1_ref, bpw1_ref, wdw_ref, bdw_ref, lng_ref, lnb_ref),
                         gext_ref, cbuf_ref, sw_ref, ns=ns, tm=tm, carry=False))
    nst_ref[...] = gext_ref[:, tm:, :]
    _run(_l0_ffn_items(x, sw_ref, (wpw2_ref, bpw2_ref, gf_ref, wg_ref, wu_ref, wd_ref), act_ref, yflat_ref))
    y_ref[...] = yflat_ref[...].reshape(ns, tm, D_MODEL)


def _head_mean_square(z):
    r = lax.broadcasted_iota(jnp.int32, (MXU_N, MXU_N), 0) // HEAD_DIM
    c = lax.broadcasted_iota(jnp.int32, (MXU_N, MXU_N), 1) // HEAD_DIM
    ones_bd = jnp.where(r == c, 1.0, 0.0).astype(BF16)
    outs = []
    for j in range(z.shape[1] // MXU_N):
        zz = z[:, j * MXU_N:(j + 1) * MXU_N]
        zz = zz * zz
        hi = zz.astype(BF16)
        lo = (zz - hi.astype(F32)).astype(BF16)
        outs.append(_dot(hi, ones_bd) + _dot(lo, ones_bd))
    ss = outs[0] if len(outs) == 1 else jnp.concatenate(outs, axis=1)
    return ss * (1.0 / HEAD_DIM)


def _init_alibi_bias(bias_ref, tq):
    rows4, nk = bias_ref.shape[1:]
    rows = lax.broadcasted_iota(jnp.int32, (rows4, nk), 0)
    cols = lax.broadcasted_iota(jnp.int32, (rows4, nk), 1)
    dist = jnp.abs((rows % tq) + WINDOW - cols).astype(F32)
    grp = rows // tq
    for kvh in range(N_KV):
        slope = jnp.zeros((rows4, nk), F32)
        for g_ in range(GROUP):
            slope = jnp.where(grp == g_, 2.0 ** (-8.0 * (kvh * GROUP + g_ + 1) / N_HEADS), slope)
        bias_ref[kvh] = -(slope * dist)


def _l1_mixer_items(x, w, caches, sinks_ref, ko_ref, vo_ref, qn_ref, kbuf_ref, vbuf_ref, ao_ref, bias_ref,
                    *, ns, tm, tq, first_pos, carry):
    (gm_ref, wqkv_ref, qg_ref, kg_ref) = w
    nk = WINDOW + tq
    rows4 = GROUP * tq
    h = _rms(x, gm_ref[...]).astype(BF16)
    q = _dot(h, wqkv_ref[:, :D_MODEL])
    k = _dot(h, wqkv_ref[:, D_MODEL:D_MODEL + KV_W])
    v = _dot(h, wqkv_ref[:, D_MODEL + KV_W:])
    qn = q * lax.rsqrt(_head_mean_square(q) + RMS_EPS) * (qg_ref[...] * SCALE)
    kn = k * lax.rsqrt(_head_mean_square(k) + RMS_EPS) * kg_ref[...]
    qn_ref[...] = qn.astype(BF16)
    kbuf_ref[:, WINDOW:, :] = kn.astype(BF16).reshape(ns, tm, KV_W)
    vbuf_ref[:, WINDOW:, :] = v.astype(BF16).reshape(ns, tm, KV_W)
    kn3 = kn.reshape(ns, tm, KV_W)
    v3 = v.reshape(ns, tm, KV_W)
    if tm >= WINDOW:
        ko_ref[...] = kn3[:, tm - WINDOW:, :]
        vo_ref[...] = v3[:, tm - WINDOW:, :]
    else:
        kc_ref, vc_ref = caches
        ko_ref[:, :WINDOW - tm, :] = kc_ref[:, tm:, :]
        vo_ref[:, :WINDOW - tm, :] = vc_ref[:, tm:, :]
        ko_ref[:, WINDOW - tm:, :] = kn3
        vo_ref[:, WINDOW - tm:, :] = v3
    yield

    grp_col = lax.broadcasted_iota(jnp.int32, (rows4, 1), 0) // tq
    sink_cols = []
    for kvh in range(N_KV):
        sc = jnp.zeros((rows4, 1), F32)
        for g_ in range(GROUP):
            sc = jnp.where(grp_col == g_, sinks_ref[kvh * GROUP + g_], sc)
        sink_cols.append(sc)
    key_col = lax.broadcasted_iota(jnp.int32, (rows4, nk), 1)

    for s in range(ns):
        for c in range(tm // tq):
            r0 = c * tq
            row0 = s * tm + r0
            for kvh in range(N_KV):
                q4 = jnp.concatenate(
                    [qn_ref[pl.ds(row0, tq), (kvh * GROUP + g_) * HEAD_DIM:(kvh * GROUP + g_ + 1) * HEAD_DIM]
                     for g_ in range(GROUP)], axis=0)
                ks = slice(kvh * HEAD_DIM, (kvh + 1) * HEAD_DIM)
                kb = kbuf_ref[s, pl.ds(r0, nk), ks]
                vb = vbuf_ref[s, pl.ds(r0, nk), ks]
                sc = lax.dot_general(q4, kb, (((1,), (1,)), ((), ())), preferred_element_type=F32)
                sc = sc + bias_ref[kvh]
                if first_pos is not None:
                    sc = jnp.where(key_col >= WINDOW - r0 - first_pos, sc, NEG)
                sink = sink_cols[kvh]
                mx = jnp.maximum(jnp.max(sc, axis=-1, keepdims=True), sink)
                p = jnp.exp(sc - mx)
                denom = jnp.sum(p, axis=-1, keepdims=True) + jnp.exp(sink - mx)
                o = _dot(p.astype(BF16), vb) / denom
                o_cat = jnp.concatenate([o[g_ * tq:(g_ + 1) * tq] for g_ in range(GROUP)], axis=1)
                ao_ref[pl.ds(row0, tq), kvh * GROUP * HEAD_DIM:(kvh + 1) * GROUP * HEAD_DIM] = o_cat.astype(BF16)
                yield
    if carry:
        kbuf_ref[:, :WINDOW, :] = kbuf_ref[:, tm:, :]
        vbuf_ref[:, :WINDOW, :] = vbuf_ref[:, tm:, :]


def _l1_mixer_count(ns, tm, tq):
    return 1 + ns * (tm // tq) * N_KV


def _l1_ffn_items(xres, ao_ref, w, act_ref, y_ref):
    (wo_ref, gf_ref, wg_ref, wu_ref, wd_ref) = w
    x1 = xres + _dot(ao_ref[...], wo_ref[...])
    yield _tok(x1)
    yield from _ffn_items(x1, gf_ref, wg_ref, wu_ref, wd_ref, act_ref, y_ref)


def _l1_ffn_then_proj_items(x, x1_ref, ao_ref, w, act_ref, y_ref, *, do_ffn=True, do_proj=True):
    (wo_ref, gf_ref, wg_ref, wu_ref, wd_ref) = w
    if do_ffn:
        yield from _ffn_items(x1_ref[...], gf_ref, wg_ref, wu_ref, wd_ref, act_ref, y_ref)
    if do_proj:
        x1 = x + _dot(ao_ref[...], wo_ref[...])
        x1_ref[...] = x1
        yield _tok(x1)


def _layer1_prompt_kernel(x_ref, sinks_ref, gm_ref, wqkv_ref, qg_ref, kg_ref, wo_ref, gf_ref,
                          wg_ref, wu_ref, wd_ref, y_ref, ko_ref, vo_ref,
                          qn_ref, kbuf_ref, vbuf_ref, ao_ref, act_ref, bias_ref, x1_ref, *, tm, n_t, n_tiles):
    i = pl.program_id(0)
    t = lax.rem(jnp.minimum(i, n_tiles - 1), n_t)

    @pl.when(i == 0)
    def _():
        _init_alibi_bias(bias_ref, CHUNK)

    @pl.when(t == 0)
    def _():
        kbuf_ref[:, :WINDOW, :] = jnp.zeros((1, WINDOW, KV_W), BF16)
        vbuf_ref[:, :WINDOW, :] = jnp.zeros((1, WINDOW, KV_W), BF16)

    def ffn(**stages):
        return _l1_ffn_then_proj_items(x_ref[0], x1_ref, ao_ref, (wo_ref, gf_ref, wg_ref, wu_ref, wd_ref),
                                       act_ref, y_ref.at[0], **stages)

    def mixer():
        return _l1_mixer_items(x_ref[0], (gm_ref, wqkv_ref, qg_ref, kg_ref), None, sinks_ref, ko_ref, vo_ref,
                               qn_ref, kbuf_ref, vbuf_ref, ao_ref, bias_ref,
                               ns=1, tm=tm, tq=CHUNK, first_pos=t * tm, carry=True)

    @pl.when(i == 0)
    def _():
        _run(mixer())
        _run(ffn(do_ffn=False))

    @pl.when((i > 0) & (i < n_tiles))
    def _():
        _interleave(ffn(), FFN_ITEMS + 1, mixer(), _l1_mixer_count(1, tm, CHUNK), side_before_last_main=True)

    @pl.when(i == n_tiles)
    def _():
        _run(ffn(do_proj=False))


def _layer1_sample_kernel(x_ref, kc_ref, vc_ref, sinks_ref, gm_ref, wqkv_ref, qg_ref, kg_ref, wo_ref, gf_ref,
                          wg_ref, wu_ref, wd_ref, y_ref, ko_ref, vo_ref,
                          qn_ref, kbuf_ref, vbuf_ref, ao_ref, act_ref, bias_ref, yflat_ref, *, ns, tm):
    m = ns * tm
    _init_alibi_bias(bias_ref, tm)
    kbuf_ref[:, :WINDOW, :] = kc_ref[...].astype(BF16)
    vbuf_ref[:, :WINDOW, :] = vc_ref[...].astype(BF16)
    x = x_ref[...].reshape(m, D_MODEL)
    _run(_l1_mixer_items(x, (gm_ref, wqkv_ref, qg_ref, kg_ref), (kc_ref, vc_ref), sinks_ref, ko_ref, vo_ref,
                         qn_ref, kbuf_ref, vbuf_ref, ao_ref, bias_ref,
                         ns=ns, tm=tm, tq=tm, first_pos=None, carry=False))
    _run(_l1_ffn_items(x, ao_ref, (wo_ref, gf_ref, wg_ref, wu_ref, wd_ref), act_ref, yflat_ref))
    y_ref[...] = yflat_ref[...].reshape(ns, tm, D_MODEL)


def _resident(arr, layer=None):
    if layer is None:
        nd = arr.ndim
        return pl.BlockSpec(arr.shape, lambda *_, _nd=nd: (0,) * _nd, pipeline_mode=pl.Buffered(1))
    nd = arr.ndim - 1
    return pl.BlockSpec((None,) + arr.shape[1:], lambda *_, _nd=nd, _l=layer: (_l,) + (0,) * _nd,
                        pipeline_mode=pl.Buffered(1))


def _params():
    return pltpu.CompilerParams(dimension_semantics=("arbitrary",), vmem_limit_bytes=VMEM_LIMIT)


def _tile_maps(n_t, n_tiles):
    def cur(i):
        j = jnp.minimum(i, n_tiles - 1)
        return (j // n_t, j % n_t, 0)

    def prev(i):
        j = jnp.maximum(i - 1, 0)
        return (j // n_t, j % n_t, 0)

    def cur_stream(i):
        return (jnp.minimum(i, n_tiles - 1) // n_t, 0, 0)

    return cur, prev, cur_stream


def _whole(shape):
    return pl.BlockSpec(shape, lambda i, _n=len(shape): (0,) * _n)


def _layer0_prompt(x, state, w0, *, tm):
    nb, seq, _ = x.shape
    n_t = seq // tm
    n_tiles = nb * n_t
    cur, prev, cur_stream = _tile_maps(n_t, n_tiles)
    tile = (1, tm, D_MODEL)
    return pl.pallas_call(
        functools.partial(_layer0_prompt_kernel, tm=tm, n_t=n_t, n_tiles=n_tiles),
        grid=(n_tiles + 1,),
        in_specs=[pl.BlockSpec(tile, cur), pl.BlockSpec(tile, prev), pl.BlockSpec((1, HALO, D_MODEL), cur_stream)]
                 + [_resident(a, l) for a, l in w0],
        out_specs=[pl.BlockSpec(tile, prev), pl.BlockSpec((1, HALO, D_MODEL), cur_stream)],
        out_shape=[jax.ShapeDtypeStruct((nb, seq, D_MODEL), F32),
                   jax.ShapeDtypeStruct((nb, HALO, D_MODEL), F32)],
        scratch_shapes=[pltpu.VMEM((1, HALO + tm, D_MODEL), F32),
                        pltpu.VMEM((tm, D_MODEL), F32),
                        pltpu.VMEM((tm, D_MODEL), BF16),
                        pltpu.VMEM((tm, D_FF), BF16)],
        compiler_params=_params(),
        name="layer0_prompt",
    )(x, x, state, *[a for a, _ in w0])


def _layer0_sample(x, state, w0):
    ns, tm, _ = x.shape
    m = ns * tm
    return pl.pallas_call(
        functools.partial(_layer0_sample_kernel, ns=ns, tm=tm),
        grid=(1,),
        in_specs=[_whole((ns, tm, D_MODEL)), _whole((ns, HALO, D_MODEL))] + [_resident(a, l) for a, l in w0],
        out_specs=[_whole((ns, tm, D_MODEL)), _whole((ns, HALO, D_MODEL))],
        out_shape=[jax.ShapeDtypeStruct((ns, tm, D_MODEL), F32),
                   jax.ShapeDtypeStruct((ns, HALO, D_MODEL), F32)],
        scratch_shapes=[pltpu.VMEM((ns, HALO + tm, D_MODEL), F32),
                        pltpu.VMEM((m, D_MODEL), F32),
                        pltpu.VMEM((m, D_MODEL), BF16),
                        pltpu.VMEM((m, D_FF), BF16),
                        pltpu.VMEM((m, D_MODEL), F32)],
        compiler_params=_params(),
        name="layer0_sample",
    )(x, state, *[a for a, _ in w0])


def _layer1_scratch(ns, tm, tq):
    m = ns * tm
    return [pltpu.VMEM((m, D_MODEL), BF16),
            pltpu.VMEM((ns, WINDOW + tm, KV_W), BF16),
            pltpu.VMEM((ns, WINDOW + tm, KV_W), BF16),
            pltpu.VMEM((m, D_MODEL), BF16),
            pltpu.VMEM((m, D_FF), BF16),
            pltpu.VMEM((N_KV, GROUP * tq, WINDOW + tq), F32),
            pltpu.VMEM((m, D_MODEL), F32)]


def _layer1_prompt(x, sinks, w1, *, tm):
    nb, seq, _ = x.shape
    n_t = seq // tm
    n_tiles = nb * n_t
    cur, prev, cur_stream = _tile_maps(n_t, n_tiles)
    tile = (1, tm, D_MODEL)
    cache = pl.BlockSpec((1, WINDOW, KV_W), cur_stream)
    return pl.pallas_call(
        functools.partial(_layer1_prompt_kernel, tm=tm, n_t=n_t, n_tiles=n_tiles),
        grid=(n_tiles + 1,),
        in_specs=[pl.BlockSpec(tile, cur), pl.BlockSpec(memory_space=pltpu.SMEM)]
                 + [_resident(a, l) for a, l in w1],
        out_specs=[pl.BlockSpec(tile, prev), cache, cache],
        out_shape=[jax.ShapeDtypeStruct((nb, seq, D_MODEL), F32),
                   jax.ShapeDtypeStruct((nb, WINDOW, KV_W), F32),
                   jax.ShapeDtypeStruct((nb, WINDOW, KV_W), F32)],
        scratch_shapes=_layer1_scratch(1, tm, CHUNK),
        compiler_params=_params(),
        name="layer1_prompt",
    )(x, sinks, *[a for a, _ in w1])


def _layer1_sample(x, kc, vc, sinks, w1):
    ns, tm, _ = x.shape
    cache = _whole((ns, WINDOW, KV_W))
    return pl.pallas_call(
        functools.partial(_layer1_sample_kernel, ns=ns, tm=tm),
        grid=(1,),
        in_specs=[_whole((ns, tm, D_MODEL)), cache, cache, pl.BlockSpec(memory_space=pltpu.SMEM)]
                 + [_resident(a, l) for a, l in w1],
        out_specs=[_whole((ns, tm, D_MODEL)), cache, cache],
        out_shape=[jax.ShapeDtypeStruct((ns, tm, D_MODEL), F32),
                   jax.ShapeDtypeStruct((ns, WINDOW, KV_W), F32),
                   jax.ShapeDtypeStruct((ns, WINDOW, KV_W), F32)],
        scratch_shapes=_layer1_scratch(ns, tm, tm),
        compiler_params=_params(),
        name="layer1_sample",
    )(x, kc, vc, sinks, *[a for a, _ in w1])


def kernel(x_prompt, x_sample, state_conv, cache_k, cache_v, g_mix, g_ffn, w_pw1, b_pw1, w_dw, b_dw,
           ln_g, ln_b, w_pw2, b_pw2, w_qkv, qn_g, kn_g, sinks, w_o, w_gate, w_up, w_down):
    nb = x_prompt.shape[0]
    ndb = x_sample.shape[0]
    row = lambda a: (a.reshape(1, -1), None)
    wg, wu, wd = w_gate.astype(BF16), w_up.astype(BF16), w_down.astype(BF16)
    w0 = [row(g_mix[0]), (w_pw1.astype(BF16), 0), row(b_pw1[0]), (w_dw, 0), row(b_dw[0]), row(ln_g[0]),
          row(ln_b[0]), (w_pw2.astype(BF16), 0), row(b_pw2[0]), row(g_ffn[0]), (wg, 0), (wu, 0), (wd, 0)]
    w1 = [row(g_mix[1]), (w_qkv.astype(BF16), 0), row(jnp.tile(qn_g[0], N_HEADS)), row(jnp.tile(kn_g[0], N_KV)),
          (w_o.astype(BF16), 0), row(g_ffn[1]), (wg, 1), (wu, 1), (wd, 1)]

    zero_state = jnp.zeros((nb, HALO, D_MODEL), F32)
    sample_state = jnp.pad(state_conv[0], ((0, 0), (HALO_PAD, 0), (0, 0)))
    xp1, cst_p = _layer0_prompt(x_prompt, zero_state, w0, tm=PROMPT_TM)
    xs1, cst_s = _layer0_sample(x_sample, sample_state, w0)

    kc = cache_k[0].reshape(ndb, WINDOW, KV_W)
    vc = cache_v[0].reshape(ndb, WINDOW, KV_W)
    yp, kp, vp = _layer1_prompt(xp1, sinks[0], w1, tm=PROMPT_TM)
    ys, ks, vs = _layer1_sample(xs1, kc, vc, sinks[0], w1)

    kv_shape = lambda a: a.reshape(1, a.shape[0], WINDOW, N_KV, HEAD_DIM)
    return (yp, ys, cst_p[None, :, HALO_PAD:, :], cst_s[None, :, HALO_PAD:, :],
            kv_shape(kp), kv_shape(vp), kv_shape(ks), kv_shape(vs))
```

```python
import functools

import jax
import jax.numpy as jnp
from jax import lax
from jax.experimental import pallas as pl
from jax.experimental.pallas import tpu as pltpu

D_MODEL = 1024
CHUNK = 64
CONV_W = 31
HEAD_DIM = 64
N_HEADS = D_MODEL // HEAD_DIM
N_KV = 4
GROUP = N_HEADS // N_KV
WINDOW = 128
D_FF = 2816
KV_W = N_KV * HEAD_DIM
SCALE = HEAD_DIM ** -0.5
NEG = -1e30
RMS_EPS = 1e-6
LN_EPS = 1e-5

SUBLANES = 8
LANES = 128
HALO = -(-(CONV_W - 1) // SUBLANES) * SUBLANES
HALO_PAD = HALO - (CONV_W - 1)
MXU_N = 256
CONV_ROWS = 64
CONV_COLS = 128
LN_ROWS = 64
VMEM_LIMIT = 56 * 1024 * 1024
PROMPT_TM = 256

F32 = jnp.float32
BF16 = jnp.bfloat16


def _dot(a, b):
    return jnp.dot(a, b, preferred_element_type=F32)


def _rms(x, g):
    ms = jnp.mean(x * x, axis=-1, keepdims=True)
    return x * lax.rsqrt(ms + RMS_EPS) * g


def _tok(v):
    return v[:SUBLANES, :LANES]


def _run(gen):
    for _ in gen:
        pass


def _interleave(main, n_main, side, n_side, side_before_last_main=False):
    next(side)
    done = 1
    for i in range(n_main):
        tok = next(main)
        if side_before_last_main:
            want = min(n_side, -(-(n_side * (i + 1)) // (n_main - 1)))
        else:
            want = (n_side * (i + 1)) // n_main
        for _ in range(max(want, 1) - done):
            side.send(tok)
        done = max(want, 1)
    _run(main)
    _run(side)


def _order_after(z, tok):
    if tok is None:
        return z
    bits = pltpu.bitcast(tok, jnp.uint32)
    bits = lax.shift_right_logical(lax.shift_right_logical(bits, jnp.uint32(16)), jnp.uint32(16))
    zero = pltpu.bitcast(bits, F32)
    return z + jnp.concatenate([zero] * (z.shape[0] // zero.shape[0]), axis=0)


FFN_ITEMS = D_FF // MXU_N + D_MODEL // MXU_N


def _ffn_items(x1, gf_ref, wg_ref, wu_ref, wd_ref, act_ref, y_ref):
    hn = _rms(x1, gf_ref[...]).astype(BF16)
    for j in range(D_FF // MXU_N):
        sl = slice(j * MXU_N, (j + 1) * MXU_N)
        gt = _dot(hn, wg_ref[:, sl])
        up = _dot(hn, wu_ref[:, sl])
        act_ref[:, sl] = (gt * jax.nn.sigmoid(gt) * up).astype(BF16)
        yield _tok(gt)
    for j in range(D_MODEL // MXU_N):
        sl = slice(j * MXU_N, (j + 1) * MXU_N)
        dn = _dot(act_ref[...], wd_ref[:, sl])
        y_ref[:, sl] = x1[:, sl] + dn
        yield _tok(dn)


def _l0_mixer_items(x, w, gext_ref, cbuf_ref, sw_ref, *, ns, tm, carry):
    (gm_ref, wpw1_ref, bpw1_ref, wdw_ref, bdw_ref, lng_ref, lnb_ref) = w
    m = ns * tm
    h = _rms(x, gm_ref[...]).astype(BF16)
    a = _dot(h, wpw1_ref[:, :D_MODEL]) + bpw1_ref[:, :D_MODEL]
    gate = _dot(h, wpw1_ref[:, D_MODEL:]) + bpw1_ref[:, D_MODEL:]
    g = a * jax.nn.sigmoid(gate)
    gext_ref[:, HALO:, :] = g.reshape(ns, tm, D_MODEL)
    tok = yield

    rb = min(tm, CONV_ROWS)
    nz = rb + HALO
    for s in range(ns):
        for r0 in range(0, tm, rb):
            for c0 in range(0, D_MODEL, CONV_COLS):
                cs = slice(c0, c0 + CONV_COLS)
                z = _order_after(gext_ref[s, pl.ds(r0, nz), cs], tok)
                acc = None
                for p in range(SUBLANES):
                    rows = rb if p == 0 else rb + SUBLANES
                    part = None
                    for k in range(CONV_W):
                        if (HALO_PAD + k) % SUBLANES != p:
                            continue
                        off = HALO_PAD + k - p
                        term = z[off:off + rows] * wdw_ref[k:k + 1, cs]
                        part = term if part is None else part + term
                    if p:
                        part = pltpu.roll(part, rows - p, axis=0)[:rb]
                    acc = part if acc is None else acc + part
                cbuf_ref[pl.ds(s * tm + r0, rb), cs] = acc + bdw_ref[:, cs]
                tok = yield
    if carry:
        gext_ref[:, :HALO, :] = gext_ref[:, tm:, :]

    lr = min(m, LN_ROWS)
    for r0 in range(0, m, lr):
        c = cbuf_ref[pl.ds(r0, lr), :]
        mu = jnp.mean(c, axis=-1, keepdims=True)
        xc = c - mu
        ln = xc * lax.rsqrt(jnp.mean(xc * xc, axis=-1, keepdims=True) + LN_EPS) * lng_ref[...] + lnb_ref[...]
        sw_ref[pl.ds(r0, lr), :] = (ln * jax.nn.sigmoid(ln)).astype(BF16)
        yield


def _l0_mixer_count(ns, tm):
    rb = min(tm, CONV_ROWS)
    return 1 + ns * (tm // rb) * (D_MODEL // CONV_COLS) + (ns * tm) // min(ns * tm, LN_ROWS)


def _l0_ffn_items(xres, sw_ref, w, act_ref, y_ref):
    (wpw2_ref, bpw2_ref, gf_ref, wg_ref, wu_ref, wd_ref) = w
    x1 = xres + _dot(sw_ref[...], wpw2_ref[...]) + bpw2_ref[...]
    yield _tok(x1)
    yield from _ffn_items(x1, gf_ref, wg_ref, wu_ref, wd_ref, act_ref, y_ref)


def _layer0_prompt_kernel(x_ref, xprev_ref, st_ref, gm_ref, wpw1_ref, bpw1_ref, wdw_ref, bdw_ref,
                          lng_ref, lnb_ref, wpw2_ref, bpw2_ref, gf_ref, wg_ref, wu_ref, wd_ref,
                          y_ref, nst_ref, gext_ref, cbuf_ref, sw_ref, act_ref, *, tm, n_t, n_tiles):
    i = pl.program_id(0)
    t = lax.rem(jnp.minimum(i, n_tiles - 1), n_t)

    @pl.when(t == 0)
    def _():
        gext_ref[:, :HALO, :] = st_ref[...]

    def ffn():
        return _l0_ffn_items(xprev_ref[0], sw_ref, (wpw2_ref, bpw2_ref, gf_ref, wg_ref, wu_ref, wd_ref),
                             act_ref, y_ref.at[0])

    def mixer():
        return _l0_mixer_items(x_ref[0], (gm_ref, wpw1_ref, bpw1_ref, wdw_ref, bdw_ref, lng_ref, lnb_ref),
                               gext_ref, cbuf_ref, sw_ref, ns=1, tm=tm, carry=True)

    @pl.when(i == 0)
    def _():
        _run(mixer())

    @pl.when((i > 0) & (i < n_tiles))
    def _():
        _interleave(ffn(), 1 + FFN_ITEMS, mixer(), _l0_mixer_count(1, tm))

    @pl.when(i == n_tiles)
    def _():
        _run(ffn())

    @pl.when((t == n_t - 1) & (i < n_tiles))
    def _():
        nst_ref[...] = gext_ref[:, tm:, :]


def _layer0_sample_kernel(x_ref, st_ref, gm_ref, wpw1_ref, bpw1_ref, wdw_ref, bdw_ref, lng_ref, lnb_ref,
                          wpw2_ref, bpw2_ref, gf_ref, wg_ref, wu_ref, wd_ref,
                          y_ref, nst_ref, gext_ref, cbuf_ref, sw_ref, act_ref, yflat_ref, *, ns, tm):
    m = ns * tm
    gext_ref[:, :HALO, :] = st_ref[...]
    x = x_ref[...].reshape(m, D_MODEL)
    _run(_l0_mixer_items(x, (gm_ref, wpw1_ref, bpw1_ref, wdw_ref, bdw_ref, lng_ref, lnb_ref),
                         gext_ref, cbuf_ref, sw_ref, ns=ns, tm=tm, carry=False))
    nst_ref[...] = gext_ref[:, tm:, :]
    _run(_l0_ffn_items(x, sw_ref, (wpw2_ref, bpw2_ref, gf_ref, wg_ref, wu_ref, wd_ref), act_ref, yflat_ref))
    y_ref[...] = yflat_ref[...].reshape(ns, tm, D_MODEL)


def _head_mean_square(z):
    r = lax.broadcasted_iota(jnp.int32, (MXU_N, MXU_N), 0) // HEAD_DIM
    c = lax.broadcasted_iota(jnp.int32, (MXU_N, MXU_N), 1) // HEAD_DIM
    ones_bd = jnp.where(r == c, 1.0, 0.0).astype(BF16)
    outs = []
    for j in range(z.shape[1] // MXU_N):
        zz = z[:, j * MXU_N:(j + 1) * MXU_N]
        zz = zz * zz
        hi = zz.astype(BF16)
        lo = (zz - hi.astype(F32)).astype(BF16)
        outs.append(_dot(hi, ones_bd) + _dot(lo, ones_bd))
    ss = outs[0] if len(outs) == 1 else jnp.concatenate(outs, axis=1)
    return ss * (1.0 / HEAD_DIM)


def _init_alibi_bias(bias_ref, tq):
    rows4, nk = bias_ref.shape[1:]
    rows = lax.broadcasted_iota(jnp.int32, (rows4, nk), 0)
    cols = lax.broadcasted_iota(jnp.int32, (rows4, nk), 1)
    dist = jnp.abs((rows % tq) + WINDOW - cols).astype(F32)
    grp = rows // tq
    for kvh in range(N_KV):
        slope = jnp.zeros((rows4, nk), F32)
        for g_ in range(GROUP):
            slope = jnp.where(grp == g_, 2.0 ** (-8.0 * (kvh * GROUP + g_ + 1) / N_HEADS), slope)
        bias_ref[kvh] = -(slope * dist)


def _l1_mixer_items(x, w, caches, sinks_ref, ko_ref, vo_ref, qn_ref, kbuf_ref, vbuf_ref, ao_ref, bias_ref,
                    *, ns, tm, tq, first_pos, carry):
    (gm_ref, wqkv_ref, qg_ref, kg_ref) = w
    nk = WINDOW + tq
    rows4 = GROUP * tq
    h = _rms(x, gm_ref[...]).astype(BF16)
    q = _dot(h, wqkv_ref[:, :D_MODEL])
    k = _dot(h, wqkv_ref[:, D_MODEL:D_MODEL + KV_W])
    v = _dot(h, wqkv_ref[:, D_MODEL + KV_W:])
    qn = q * lax.rsqrt(_head_mean_square(q) + RMS_EPS) * (qg_ref[...] * SCALE)
    kn = k * lax.rsqrt(_head_mean_square(k) + RMS_EPS) * kg_ref[...]
    qn_ref[...] = qn.astype(BF16)
    kbuf_ref[:, WINDOW:, :] = kn.astype(BF16).reshape(ns, tm, KV_W)
    vbuf_ref[:, WINDOW:, :] = v.astype(BF16).reshape(ns, tm, KV_W)
    kn3 = kn.reshape(ns, tm, KV_W)
    v3 = v.reshape(ns, tm, KV_W)
    if tm >= WINDOW:
        ko_ref[...] = kn3[:, tm - WINDOW:, :]
        vo_ref[...] = v3[:, tm - WINDOW:, :]
    else:
        kc_ref, vc_ref = caches
        ko_ref[:, :WINDOW - tm, :] = kc_ref[:, tm:, :]
        vo_ref[:, :WINDOW - tm, :] = vc_ref[:, tm:, :]
        ko_ref[:, WINDOW - tm:, :] = kn3
        vo_ref[:, WINDOW - tm:, :] = v3
    yield

    grp_col = lax.broadcasted_iota(jnp.int32, (rows4, 1), 0) // tq
    sink_cols = []
    for kvh in range(N_KV):
        sc = jnp.zeros((rows4, 1), F32)
        for g_ in range(GROUP):
            sc = jnp.where(grp_col == g_, sinks_ref[kvh * GROUP + g_], sc)
        sink_cols.append(sc)
    key_col = lax.broadcasted_iota(jnp.int32, (rows4, nk), 1)

    for s in range(ns):
        for c in range(tm // tq):
            r0 = c * tq
            row0 = s * tm + r0
            for kvh in range(N_KV):
                q4 = jnp.concatenate(
                    [qn_ref[pl.ds(row0, tq), (kvh * GROUP + g_) * HEAD_DIM:(kvh * GROUP + g_ + 1) * HEAD_DIM]
                     for g_ in range(GROUP)], axis=0)
                ks = slice(kvh * HEAD_DIM, (kvh + 1) * HEAD_DIM)
                kb = kbuf_ref[s, pl.ds(r0, nk), ks]
                vb = vbuf_ref[s, pl.ds(r0, nk), ks]
                sc = lax.dot_general(q4, kb, (((1,), (1,)), ((), ())), preferred_element_type=F32)
                sc = sc + bias_ref[kvh]
                if first_pos is not None:
                    sc = jnp.where(key_col >= WINDOW - r0 - first_pos, sc, NEG)
                sink = sink_cols[kvh]
                mx = jnp.maximum(jnp.max(sc, axis=-1, keepdims=True), sink)
                p = jnp.exp(sc - mx)
                denom = jnp.sum(p, axis=-1, keepdims=True) + jnp.exp(sink - mx)
                o = _dot(p.astype(BF16), vb) / denom
                o_cat = jnp.concatenate([o[g_ * tq:(g_ + 1) * tq] for g_ in range(GROUP)], axis=1)
                ao_ref[pl.ds(row0, tq), kvh * GROUP * HEAD_DIM:(kvh + 1) * GROUP * HEAD_DIM] = o_cat.astype(BF16)
                yield
    if carry:
        kbuf_ref[:, :WINDOW, :] = kbuf_ref[:, tm:, :]
        vbuf_ref[:, :WINDOW, :] = vbuf_ref[:, tm:, :]


def _l1_mixer_count(ns, tm, tq):
    return 1 + ns * (tm // tq) * N_KV


def _l1_ffn_items(xres, ao_ref, w, act_ref, y_ref):
    (wo_ref, gf_ref, wg_ref, wu_ref, wd_ref) = w
    x1 = xres + _dot(ao_ref[...], wo_ref[...])
    yield _tok(x1)
    yield from _ffn_items(x1, gf_ref, wg_ref, wu_ref, wd_ref, act_ref, y_ref)


def _l1_ffn_then_proj_items(x, x1_ref, ao_ref, w, act_ref, y_ref, *, do_ffn=True, do_proj=True):
    (wo_ref, gf_ref, wg_ref, wu_ref, wd_ref) = w
    if do_ffn:
        yield from _ffn_items(x1_ref[...], gf_ref, wg_ref, wu_ref, wd_ref, act_ref, y_ref)
    if do_proj:
        x1 = x + _dot(ao_ref[...], wo_ref[...])
        x1_ref[...] = x1
        yield _tok(x1)


def _layer1_prompt_kernel(x_ref, sinks_ref, gm_ref, wqkv_ref, qg_ref, kg_ref, wo_ref, gf_ref,
                          wg_ref, wu_ref, wd_ref, y_ref, ko_ref, vo_ref,
                          qn_ref, kbuf_ref, vbuf_ref, ao_ref, act_ref, bias_ref, x1_ref, *, tm, n_t, n_tiles):
    i = pl.program_id(0)
    t = lax.rem(jnp.minimum(i, n_tiles - 1), n_t)

    @pl.when(i == 0)
    def _():
        _init_alibi_bias(bias_ref, CHUNK)

    @pl.when(t == 0)
    def _():
        kbuf_ref[:, :WINDOW, :] = jnp.zeros((1, WINDOW, KV_W), BF16)
        vbuf_ref[:, :WINDOW, :] = jnp.zeros((1, WINDOW, KV_W), BF16)

    def ffn(**stages):
        return _l1_ffn_then_proj_items(x_ref[0], x1_ref, ao_ref, (wo_ref, gf_ref, wg_ref, wu_ref, wd_ref),
                                       act_ref, y_ref.at[0], **stages)

    def mixer():
        return _l1_mixer_items(x_ref[0], (gm_ref, wqkv_ref, qg_ref, kg_ref), None, sinks_ref, ko_ref, vo_ref,
                               qn_ref, kbuf_ref, vbuf_ref, ao_ref, bias_ref,
                               ns=1, tm=tm, tq=CHUNK, first_pos=t * tm, carry=True)

    @pl.when(i == 0)
    def _():
        _run(mixer())
        _run(ffn(do_ffn=False))

    @pl.when((i > 0) & (i < n_tiles))
    def _():
        _interleave(ffn(), FFN_ITEMS + 1, mixer(), _l1_mixer_count(1, tm, CHUNK), side_before_last_main=True)

    @pl.when(i == n_tiles)
    def _():
        _run(ffn(do_proj=False))


def _layer1_sample_kernel(x_ref, kc_ref, vc_ref, sinks_ref, gm_ref, wqkv_ref, qg_ref, kg_ref, wo_ref, gf_ref,
                          wg_ref, wu_ref, wd_ref, y_ref, ko_ref, vo_ref,
                          qn_ref, kbuf_ref, vbuf_ref, ao_ref, act_ref, bias_ref, yflat_ref, *, ns, tm):
    m = ns * tm
    _init_alibi_bias(bias_ref, tm)
    kbuf_ref[:, :WINDOW, :] = kc_ref[...].astype(BF16)
    vbuf_ref[:, :WINDOW, :] = vc_ref[...].astype(BF16)
    x = x_ref[...].reshape(m, D_MODEL)
    _run(_l1_mixer_items(x, (gm_ref, wqkv_ref, qg_ref, kg_ref), (kc_ref, vc_ref), sinks_ref, ko_ref, vo_ref,
                         qn_ref, kbuf_ref, vbuf_ref, ao_ref, bias_ref,
                         ns=ns, tm=tm, tq=tm, first_pos=None, carry=False))
    _run(_l1_ffn_items(x, ao_ref, (wo_ref, gf_ref, wg_ref, wu_ref, wd_ref), act_ref, yflat_ref))
    y_ref[...] = yflat_ref[...].reshape(ns, tm, D_MODEL)


def _resident(arr, layer=None):
    if layer is None:
        nd = arr.ndim
        return pl.BlockSpec(arr.shape, lambda *_, _nd=nd: (0,) * _nd, pipeline_mode=pl.Buffered(1))
    nd = arr.ndim - 1
    return pl.BlockSpec((None,) + arr.shape[1:], lambda *_, _nd=nd, _l=layer: (_l,) + (0,) * _nd,
                        pipeline_mode=pl.Buffered(1))


def _params():
    return pltpu.CompilerParams(dimension_semantics=("arbitrary",), vmem_limit_bytes=VMEM_LIMIT)


def _tile_maps(n_t, n_tiles):
    def cur(i):
        j = jnp.minimum(i, n_tiles - 1)
        return (j // n_t, j % n_t, 0)

    def prev(i):
        j = jnp.maximum(i - 1, 0)
        return (j // n_t, j % n_t, 0)

    def cur_stream(i):
        return (jnp.minimum(i, n_tiles - 1) // n_t, 0, 0)

    return cur, prev, cur_stream


def _whole(shape):
    return pl.BlockSpec(shape, lambda i, _n=len(shape): (0,) * _n)


def _layer0_prompt(x, state, w0, *, tm):
    nb, seq, _ = x.shape
    n_t = seq // tm
    n_tiles = nb * n_t
    cur, prev, cur_stream = _tile_maps(n_t, n_tiles)
    tile = (1, tm, D_MODEL)
    return pl.pallas_call(
        functools.partial(_layer0_prompt_kernel, tm=tm, n_t=n_t, n_tiles=n_tiles),
        grid=(n_tiles + 1,),
        in_specs=[pl.BlockSpec(tile, cur), pl.BlockSpec(tile, prev), pl.BlockSpec((1, HALO, D_MODEL), cur_stream)]
                 + [_resident(a, l) for a, l in w0],
        out_specs=[pl.BlockSpec(tile, prev), pl.BlockSpec((1, HALO, D_MODEL), cur_stream)],
        out_shape=[jax.ShapeDtypeStruct((nb, seq, D_MODEL), F32),
                   jax.ShapeDtypeStruct((nb, HALO, D_MODEL), F32)],
        scratch_shapes=[pltpu.VMEM((1, HALO + tm, D_MODEL), F32),
                        pltpu.VMEM((tm, D_MODEL), F32),
                        pltpu.VMEM((tm, D_MODEL), BF16),
                        pltpu.VMEM((tm, D_FF), BF16)],
        compiler_params=_params(),
        name="layer0_prompt",
    )(x, x, state, *[a for a, _ in w0])


def _layer0_sample(x, state, w0):
    ns, tm, _ = x.shape
    m = ns * tm
    return pl.pallas_call(
        functools.partial(_layer0_sample_kernel, ns=ns, tm=tm),
        grid=(1,),
        in_specs=[_whole((ns, tm, D_MODEL)), _whole((ns, HALO, D_MODEL))] + [_resident(a, l) for a, l in w0],
        out_specs=[_whole((ns, tm, D_MODEL)), _whole((ns, HALO, D_MODEL))],
        out_shape=[jax.ShapeDtypeStruct((ns, tm, D_MODEL), F32),
                   jax.ShapeDtypeStruct((ns, HALO, D_MODEL), F32)],
        scratch_shapes=[pltpu.VMEM((ns, HALO + tm, D_MODEL), F32),
                        pltpu.VMEM((m, D_MODEL), F32),
                        pltpu.VMEM((m, D_MODEL), BF16),
                        pltpu.VMEM((m, D_FF), BF16),
                        pltpu.VMEM((m, D_MODEL), F32)],
        compiler_params=_params(),
        name="layer0_sample",
    )(x, state, *[a for a, _ in w0])


def _layer1_scratch(ns, tm, tq):
    m = ns * tm
    return [pltpu.VMEM((m, D_MODEL), BF16),
            pltpu.VMEM((ns, WINDOW + tm, KV_W), BF16),
            pltpu.VMEM((ns, WINDOW + tm, KV_W), BF16),
            pltpu.VMEM((m, D_MODEL), BF16),
            pltpu.VMEM((m, D_FF), BF16),
            pltpu.VMEM((N_KV, GROUP * tq, WINDOW + tq), F32),
            pltpu.VMEM((m, D_MODEL), F32)]


def _layer1_prompt(x, sinks, w1, *, tm):
    nb, seq, _ = x.shape
    n_t = seq // tm
    n_tiles = nb * n_t
    cur, prev, cur_stream = _tile_maps(n_t, n_tiles)
    tile = (1, tm, D_MODEL)
    cache = pl.BlockSpec((1, WINDOW, KV_W), cur_stream)
    return pl.pallas_call(
        functools.partial(_layer1_prompt_kernel, tm=tm, n_t=n_t, n_tiles=n_tiles),
        grid=(n_tiles + 1,),
        in_specs=[pl.BlockSpec(tile, cur), pl.BlockSpec(memory_space=pltpu.SMEM)]
                 + [_resident(a, l) for a, l in w1],
        out_specs=[pl.BlockSpec(tile, prev), cache, cache],
        out_shape=[jax.ShapeDtypeStruct((nb, seq, D_MODEL), F32),
                   jax.ShapeDtypeStruct((nb, WINDOW, KV_W), F32),
                   jax.ShapeDtypeStruct((nb, WINDOW, KV_W), F32)],
        scratch_shapes=_layer1_scratch(1, tm, CHUNK),
        compiler_params=_params(),
        name="layer1_prompt",
    )(x, sinks, *[a for a, _ in w1])


def _layer1_sample(x, kc, vc, sinks, w1):
    ns, tm, _ = x.shape
    cache = _whole((ns, WINDOW, KV_W))
    return pl.pallas_call(
        functools.partial(_layer1_sample_kernel, ns=ns, tm=tm),
        grid=(1,),
        in_specs=[_whole((ns, tm, D_MODEL)), cache, cache, pl.BlockSpec(memory_space=pltpu.SMEM)]
                 + [_resident(a, l) for a, l in w1],
        out_specs=[_whole((ns, tm, D_MODEL)), cache, cache],
        out_shape=[jax.ShapeDtypeStruct((ns, tm, D_MODEL), F32),
                   jax.ShapeDtypeStruct((ns, WINDOW, KV_W), F32),
                   jax.ShapeDtypeStruct((ns, WINDOW, KV_W), F32)],
        scratch_shapes=_layer1_scratch(ns, tm, tm),
        compiler_params=_params(),
        name="layer1_sample",
    )(x, kc, vc, sinks, *[a for a, _ in w1])


def kernel(x_prompt, x_sample, state_conv, cache_k, cache_v, g_mix, g_ffn, w_pw1, b_pw1, w_dw, b_dw,
           ln_g, ln_b, w_pw2, b_pw2, w_qkv, qn_g, kn_g, sinks, w_o, w_gate, w_up, w_down):
    nb = x_prompt.shape[0]
    ndb = x_sample.shape[0]
    row = lambda a: (a.reshape(1, -1), None)
    wg, wu, wd = w_gate.astype(BF16), w_up.astype(BF16), w_down.astype(BF16)
    w0 = [row(g_mix[0]), (w_pw1.astype(BF16), 0), row(b_pw1[0]), (w_dw, 0), row(b_dw[0]), row(ln_g[0]),
          row(ln_b[0]), (w_pw2.astype(BF16), 0), row(b_pw2[0]), row(g_ffn[0]), (wg, 0), (wu, 0), (wd, 0)]
    w1 = [row(g_mix[1]), (w_qkv.astype(BF16), 0), row(jnp.tile(qn_g[0], N_HEADS)), row(jnp.tile(kn_g[0], N_KV)),
          (w_o.astype(BF16), 0), row(g_ffn[1]), (wg, 1), (wu, 1), (wd, 1)]

    zero_state = jnp.zeros((nb, HALO, D_MODEL), F32)
    sample_state = jnp.pad(state_conv[0], ((0, 0), (HALO_PAD, 0), (0, 0)))
    xp1, cst_p = _layer0_prompt(x_prompt, zero_state, w0, tm=PROMPT_TM)
    xs1, cst_s = _layer0_sample(x_sample, sample_state, w0)

    kc = cache_k[0].reshape(ndb, WINDOW, KV_W)
    vc = cache_v[0].reshape(ndb, WINDOW, KV_W)
    yp, kp, vp = _layer1_prompt(xp1, sinks[0], w1, tm=PROMPT_TM)
    ys, ks, vs = _layer1_sample(xs1, kc, vc, sinks[0], w1)

    kv_shape = lambda a: a.reshape(1, a.shape[0], WINDOW, N_KV, HEAD_DIM)
    return (yp, ys, cst_p[None, :, HALO_PAD:, :], cst_s[None, :, HALO_PAD:, :],
            kv_shape(kp), kv_shape(vp), kv_shape(ks), kv_shape(vs))
```

```python
import functools

import jax
import jax.numpy as jnp
from jax import lax
from jax.experimental import pallas as pl
from jax.experimental.pallas import tpu as pltpu

D_MODEL = 1024
CHUNK = 64
CONV_W = 31
HEAD_DIM = 64
N_HEADS = D_MODEL // HEAD_DIM
N_KV = 4
GROUP = N_HEADS // N_KV
WINDOW = 128
D_FF = 2816
KV_W = N_KV * HEAD_DIM
SCALE = HEAD_DIM ** -0.5
NEG = -1e30
RMS_EPS = 1e-6
LN_EPS = 1e-5

SUBLANES = 8
HALO = -(-(CONV_W - 1) // SUBLANES) * SUBLANES
HALO_PAD = HALO - (CONV_W - 1)
MXU_N = 256
CONV_ROWS = 64
CONV_COLS = 256
BF16_ROWS = 16
LN_ROWS = 64
VMEM_LIMIT = 56 * 1024 * 1024
PROMPT_TM = 512

F32 = jnp.float32
BF16 = jnp.bfloat16


def _dot(a, b):
    return jnp.dot(a, b, preferred_element_type=F32)


def _rms(x, g):
    ms = jnp.mean(x * x, axis=-1, keepdims=True)
    return x * lax.rsqrt(ms + RMS_EPS) * g


def _run(gen):
    for _ in gen:
        pass


def _interleave(main, n_main, side, n_side):
    next(side)
    done = 1
    for i in range(n_main):
        next(main)
        want = max(1, min(n_side, -(-(n_side * (i + 1)) // (n_main - 1))))
        for _ in range(want - done):
            next(side)
        done = want
    _run(main)
    _run(side)


FFN_ITEMS = D_FF // MXU_N + D_MODEL // MXU_N


def _ffn_items(x1, gf_ref, wg_ref, wu_ref, wd_ref, act_ref, y_ref):
    hn = _rms(x1, gf_ref[...]).astype(BF16)
    for j in range(D_FF // MXU_N):
        sl = slice(j * MXU_N, (j + 1) * MXU_N)
        gt = _dot(hn, wg_ref[:, sl])
        up = _dot(hn, wu_ref[:, sl])
        act_ref[:, sl] = (gt * jax.nn.sigmoid(gt) * up).astype(BF16)
        yield
    for j in range(D_MODEL // MXU_N):
        sl = slice(j * MXU_N, (j + 1) * MXU_N)
        dn = _dot(act_ref[...], wd_ref[:, sl])
        y_ref[:, sl] = x1[:, sl] + dn
        yield


def _init_conv_tables(wdw_ref, wb_ref, shift_ref):
    for k in range(CONV_W):
        wb_ref[k] = jnp.broadcast_to(wdw_ref[k:k + 1, :], (BF16_ROWS, D_MODEL)).astype(BF16)
    rb, cols = shift_ref.shape
    pr = cols // SUBLANES
    r = lax.broadcasted_iota(jnp.int32, (rb, cols), 0)
    c = lax.broadcasted_iota(jnp.int32, (rb, cols), 1)
    shift_ref[...] = jnp.where(c % pr == r + c // pr, 1.0, 0.0).astype(BF16)


def _l0_mixer(x, w, gext_ref, cbuf_ref, sw_ref, wb_ref, shift_ref, *, ns, tm, carry):
    (gm_ref, wpw1_ref, bpw1_ref, bdw_ref, lng_ref, lnb_ref) = w
    m = ns * tm
    h = _rms(x, gm_ref[...]).astype(BF16)
    a = _dot(h, wpw1_ref[:, :D_MODEL]) + bpw1_ref[:, :D_MODEL]
    gate = _dot(h, wpw1_ref[:, D_MODEL:]) + bpw1_ref[:, D_MODEL:]
    g = a * jax.nn.sigmoid(gate)
    gext_ref[:, HALO:HALO + tm, :] = g.reshape(ns, tm, D_MODEL)

    rb = min(tm, CONV_ROWS)
    pr = rb + BF16_ROWS
    nz = rb + HALO + BF16_ROWS
    for s in range(ns):
        for r0 in range(0, tm, rb):
            for c0 in range(0, D_MODEL, CONV_COLS):
                cs = slice(c0, c0 + CONV_COLS)
                z = gext_ref[s, pl.ds(r0, nz), cs]
                zb = (z.astype(BF16), z[SUBLANES:nz - SUBLANES].astype(BF16))
                parts = []
                for p in range(SUBLANES):
                    part = None
                    for k in range(CONV_W):
                        if (HALO_PAD + k) % SUBLANES != p:
                            continue
                        odd = ((HALO_PAD + k - p) // SUBLANES) % 2
                        off = HALO_PAD + k - p - SUBLANES * odd
                        wt = jnp.concatenate([wb_ref[k, :, cs]] * (pr // BF16_ROWS), axis=0)
                        term = zb[odd][off:off + pr] * wt
                        part = term if part is None else part + term
                    parts.append(part)
                acc = _dot(shift_ref[...], jnp.concatenate(parts, axis=0))
                cbuf_ref[pl.ds(s * tm + r0, rb), cs] = acc + bdw_ref[:, cs]
    if carry:
        gext_ref[:, :HALO, :] = gext_ref[:, tm:tm + HALO, :]

    lr = min(m, LN_ROWS)
    for r0 in range(0, m, lr):
        c = cbuf_ref[pl.ds(r0, lr), :]
        mu = jnp.mean(c, axis=-1, keepdims=True)
        xc = c - mu
        ln = xc * lax.rsqrt(jnp.mean(xc * xc, axis=-1, keepdims=True) + LN_EPS) * lng_ref[...] + lnb_ref[...]
        sw_ref[pl.ds(r0, lr), :] = (ln * jax.nn.sigmoid(ln)).astype(BF16)


def _l0_ffn(xres, sw_ref, w, act_ref, y_ref):
    (wpw2_ref, bpw2_ref, gf_ref, wg_ref, wu_ref, wd_ref) = w
    x1 = xres + _dot(sw_ref[...], wpw2_ref[...]) + bpw2_ref[...]
    _run(_ffn_items(x1, gf_ref, wg_ref, wu_ref, wd_ref, act_ref, y_ref))


def _layer0_prompt_kernel(x_ref, st_ref, gm_ref, wpw1_ref, bpw1_ref, wdw_ref, bdw_ref,
                          lng_ref, lnb_ref, wpw2_ref, bpw2_ref, gf_ref, wg_ref, wu_ref, wd_ref,
                          y_ref, nst_ref, gext_ref, cbuf_ref, sw_ref, act_ref, wb_ref, shift_ref, *, tm, n_t):
    i = pl.program_id(0)
    t = lax.rem(i, n_t)

    @pl.when(i == 0)
    def _():
        _init_conv_tables(wdw_ref, wb_ref, shift_ref)
        gext_ref[:, HALO + tm:, :] = jnp.zeros((1, BF16_ROWS, D_MODEL), F32)

    @pl.when(t == 0)
    def _():
        gext_ref[:, :HALO, :] = st_ref[...]

    x = x_ref[0]
    _l0_mixer(x, (gm_ref, wpw1_ref, bpw1_ref, bdw_ref, lng_ref, lnb_ref), gext_ref, cbuf_ref, sw_ref, wb_ref,
              shift_ref, ns=1, tm=tm, carry=True)
    _l0_ffn(x, sw_ref, (wpw2_ref, bpw2_ref, gf_ref, wg_ref, wu_ref, wd_ref), act_ref, y_ref.at[0])

    @pl.when(t == n_t - 1)
    def _():
        nst_ref[...] = gext_ref[:, tm:tm + HALO, :]


def _layer0_sample_kernel(x_ref, st_ref, gm_ref, wpw1_ref, bpw1_ref, wdw_ref, bdw_ref, lng_ref, lnb_ref,
                          wpw2_ref, bpw2_ref, gf_ref, wg_ref, wu_ref, wd_ref,
                          y_ref, nst_ref, gext_ref, cbuf_ref, sw_ref, act_ref, yflat_ref, wb_ref, shift_ref,
                          *, ns, tm):
    m = ns * tm
    _init_conv_tables(wdw_ref, wb_ref, shift_ref)
    gext_ref[:, HALO + tm:, :] = jnp.zeros((ns, BF16_ROWS, D_MODEL), F32)
    gext_ref[:, :HALO, :] = st_ref[...]
    x = x_ref[...].reshape(m, D_MODEL)
    _l0_mixer(x, (gm_ref, wpw1_ref, bpw1_ref, bdw_ref, lng_ref, lnb_ref), gext_ref, cbuf_ref, sw_ref, wb_ref,
              shift_ref, ns=ns, tm=tm, carry=False)
    nst_ref[...] = gext_ref[:, tm:tm + HALO, :]
    _l0_ffn(x, sw_ref, (wpw2_ref, bpw2_ref, gf_ref, wg_ref, wu_ref, wd_ref), act_ref, yflat_ref)
    y_ref[...] = yflat_ref[...].reshape(ns, tm, D_MODEL)


def _head_mean_square(z):
    r = lax.broadcasted_iota(jnp.int32, (MXU_N, MXU_N), 0) // HEAD_DIM
    c = lax.broadcasted_iota(jnp.int32, (MXU_N, MXU_N), 1) // HEAD_DIM
    ones_bd = jnp.where(r == c, 1.0, 0.0).astype(BF16)
    outs = []
    for j in range(z.shape[1] // MXU_N):
        zz = z[:, j * MXU_N:(j + 1) * MXU_N]
        zz = zz * zz
        hi = zz.astype(BF16)
        lo = (zz - hi.astype(F32)).astype(BF16)
        outs.append(_dot(hi, ones_bd) + _dot(lo, ones_bd))
    ss = outs[0] if len(outs) == 1 else jnp.concatenate(outs, axis=1)
    return ss * (1.0 / HEAD_DIM)


def _init_alibi_bias(bias_ref, tq):
    rows4, nk = bias_ref.shape[1:]
    rows = lax.broadcasted_iota(jnp.int32, (rows4, nk), 0)
    cols = lax.broadcasted_iota(jnp.int32, (rows4, nk), 1)
    dist = jnp.abs((rows % tq) + WINDOW - cols).astype(F32)
    grp = rows // tq
    for kvh in range(N_KV):
        slope = jnp.zeros((rows4, nk), F32)
        for g_ in range(GROUP):
            slope = jnp.where(grp == g_, 2.0 ** (-8.0 * (kvh * GROUP + g_ + 1) / N_HEADS), slope)
        bias_ref[kvh] = -(slope * dist)


def _l1_mixer_items(x, w, caches, sinks_ref, ko_ref, vo_ref, qn_ref, kbuf_ref, vbuf_ref, ao_ref, bias_ref,
                    *, ns, tm, tq, first_pos, carry):
    (gm_ref, wqkv_ref, qg_ref, kg_ref) = w
    nk = WINDOW + tq
    rows4 = GROUP * tq
    h = _rms(x, gm_ref[...]).astype(BF16)
    q = _dot(h, wqkv_ref[:, :D_MODEL])
    k = _dot(h, wqkv_ref[:, D_MODEL:D_MODEL + KV_W])
    v = _dot(h, wqkv_ref[:, D_MODEL + KV_W:])
    qn = q * lax.rsqrt(_head_mean_square(q) + RMS_EPS) * (qg_ref[...] * SCALE)
    kn = k * lax.rsqrt(_head_mean_square(k) + RMS_EPS) * kg_ref[...]
    qn_ref[...] = qn.astype(BF16)
    kbuf_ref[:, WINDOW:, :] = kn.astype(BF16).reshape(ns, tm, KV_W)
    vbuf_ref[:, WINDOW:, :] = v.astype(BF16).reshape(ns, tm, KV_W)
    kn3 = kn.reshape(ns, tm, KV_W)
    v3 = v.reshape(ns, tm, KV_W)
    if tm >= WINDOW:
        ko_ref[...] = kn3[:, tm - WINDOW:, :]
        vo_ref[...] = v3[:, tm - WINDOW:, :]
    else:
        kc_ref, vc_ref = caches
        ko_ref[:, :WINDOW - tm, :] = kc_ref[:, tm:, :]
        vo_ref[:, :WINDOW - tm, :] = vc_ref[:, tm:, :]
        ko_ref[:, WINDOW - tm:, :] = kn3
        vo_ref[:, WINDOW - tm:, :] = v3
    yield

    grp_col = lax.broadcasted_iota(jnp.int32, (rows4, 1), 0) // tq
    sink_cols = []
    for kvh in range(N_KV):
        sc = jnp.zeros((rows4, 1), F32)
        for g_ in range(GROUP):
            sc = jnp.where(grp_col == g_, sinks_ref[kvh * GROUP + g_], sc)
        sink_cols.append(sc)
    key_col = lax.broadcasted_iota(jnp.int32, (rows4, nk), 1)

    for s in range(ns):
        for c in range(tm // tq):
            r0 = c * tq
            row0 = s * tm + r0
            for kvh in range(N_KV):
                q4 = jnp.concatenate(
                    [qn_ref[pl.ds(row0, tq), (kvh * GROUP + g_) * HEAD_DIM:(kvh * GROUP + g_ + 1) * HEAD_DIM]
                     for g_ in range(GROUP)], axis=0)
                ks = slice(kvh * HEAD_DIM, (kvh + 1) * HEAD_DIM)
                kb = kbuf_ref[s, pl.ds(r0, nk), ks]
                vb = vbuf_ref[s, pl.ds(r0, nk), ks]
                sc = lax.dot_general(q4, kb, (((1,), (1,)), ((), ())), preferred_element_type=F32)
                sc = sc + bias_ref[kvh]
                if first_pos is not None:
                    sc = jnp.where(key_col >= WINDOW - r0 - first_pos, sc, NEG)
                sink = sink_cols[kvh]
                mx = jnp.maximum(jnp.max(sc, axis=-1, keepdims=True), sink)
                p = jnp.exp(sc - mx)
                denom = jnp.sum(p, axis=-1, keepdims=True) + jnp.exp(sink - mx)
                o = _dot(p.astype(BF16), vb) / denom
                o_cat = jnp.concatenate([o[g_ * tq:(g_ + 1) * tq] for g_ in range(GROUP)], axis=1)
                ao_ref[pl.ds(row0, tq), kvh * GROUP * HEAD_DIM:(kvh + 1) * GROUP * HEAD_DIM] = o_cat.astype(BF16)
                yield
    if carry:
        kbuf_ref[:, :WINDOW, :] = kbuf_ref[:, tm:, :]
        vbuf_ref[:, :WINDOW, :] = vbuf_ref[:, tm:, :]


def _l1_mixer_count(ns, tm, tq):
    return 1 + ns * (tm // tq) * N_KV


def _l1_ffn_items(xres, ao_ref, w, act_ref, y_ref):
    (wo_ref, gf_ref, wg_ref, wu_ref, wd_ref) = w
    x1 = xres + _dot(ao_ref[...], wo_ref[...])
    yield
    yield from _ffn_items(x1, gf_ref, wg_ref, wu_ref, wd_ref, act_ref, y_ref)


def _l1_ffn_then_proj_items(x, x1_ref, ao_ref, w, act_ref, y_ref, *, do_ffn=True, do_proj=True):
    (wo_ref, gf_ref, wg_ref, wu_ref, wd_ref) = w
    if do_ffn:
        yield from _ffn_items(x1_ref[...], gf_ref, wg_ref, wu_ref, wd_ref, act_ref, y_ref)
    if do_proj:
        x1 = x + _dot(ao_ref[...], wo_ref[...])
        x1_ref[...] = x1
        yield


def _layer1_prompt_kernel(x_ref, sinks_ref, gm_ref, wqkv_ref, qg_ref, kg_ref, wo_ref, gf_ref,
                          wg_ref, wu_ref, wd_ref, y_ref, ko_ref, vo_ref,
                          qn_ref, kbuf_ref, vbuf_ref, ao_ref, act_ref, bias_ref, x1_ref, *, tm, n_t, n_tiles):
    i = pl.program_id(0)
    t = lax.rem(jnp.minimum(i, n_tiles - 1), n_t)

    @pl.when(i == 0)
    def _():
        _init_alibi_bias(bias_ref, CHUNK)

    @pl.when(t == 0)
    def _():
        kbuf_ref[:, :WINDOW, :] = jnp.zeros((1, WINDOW, KV_W), BF16)
        vbuf_ref[:, :WINDOW, :] = jnp.zeros((1, WINDOW, KV_W), BF16)

    def ffn(**stages):
        return _l1_ffn_then_proj_items(x_ref[0], x1_ref, ao_ref, (wo_ref, gf_ref, wg_ref, wu_ref, wd_ref),
                                       act_ref, y_ref.at[0], **stages)

    def mixer():
        return _l1_mixer_items(x_ref[0], (gm_ref, wqkv_ref, qg_ref, kg_ref), None, sinks_ref, ko_ref, vo_ref,
                               qn_ref, kbuf_ref, vbuf_ref, ao_ref, bias_ref,
                               ns=1, tm=tm, tq=CHUNK, first_pos=t * tm, carry=True)

    @pl.when(i == 0)
    def _():
        _run(mixer())
        _run(ffn(do_ffn=False))

    @pl.when((i > 0) & (i < n_tiles))
    def _():
        _interleave(ffn(), FFN_ITEMS + 1, mixer(), _l1_mixer_count(1, tm, CHUNK))

    @pl.when(i == n_tiles)
    def _():
        _run(ffn(do_proj=False))


def _layer1_sample_kernel(x_ref, kc_ref, vc_ref, sinks_ref, gm_ref, wqkv_ref, qg_ref, kg_ref, wo_ref, gf_ref,
                          wg_ref, wu_ref, wd_ref, y_ref, ko_ref, vo_ref,
                          qn_ref, kbuf_ref, vbuf_ref, ao_ref, act_ref, bias_ref, yflat_ref, *, ns, tm):
    m = ns * tm
    _init_alibi_bias(bias_ref, tm)
    kbuf_ref[:, :WINDOW, :] = kc_ref[...].astype(BF16)
    vbuf_ref[:, :WINDOW, :] = vc_ref[...].astype(BF16)
    x = x_ref[...].reshape(m, D_MODEL)
    _run(_l1_mixer_items(x, (gm_ref, wqkv_ref, qg_ref, kg_ref), (kc_ref, vc_ref), sinks_ref, ko_ref, vo_ref,
                         qn_ref, kbuf_ref, vbuf_ref, ao_ref, bias_ref,
                         ns=ns, tm=tm, tq=tm, first_pos=None, carry=False))
    _run(_l1_ffn_items(x, ao_ref, (wo_ref, gf_ref, wg_ref, wu_ref, wd_ref), act_ref, yflat_ref))
    y_ref[...] = yflat_ref[...].reshape(ns, tm, D_MODEL)


def _resident(arr, layer=None):
    if layer is None:
        nd = arr.ndim
        return pl.BlockSpec(arr.shape, lambda *_, _nd=nd: (0,) * _nd, pipeline_mode=pl.Buffered(1))
    nd = arr.ndim - 1
    return pl.BlockSpec((None,) + arr.shape[1:], lambda *_, _nd=nd, _l=layer: (_l,) + (0,) * _nd,
                        pipeline_mode=pl.Buffered(1))


def _params():
    return pltpu.CompilerParams(dimension_semantics=("arbitrary",), vmem_limit_bytes=VMEM_LIMIT)


def _tile_maps(n_t, n_tiles):
    def cur(i):
        j = jnp.minimum(i, n_tiles - 1)
        return (j // n_t, j % n_t, 0)

    def prev(i):
        j = jnp.maximum(i - 1, 0)
        return (j // n_t, j % n_t, 0)

    def cur_stream(i):
        return (jnp.minimum(i, n_tiles - 1) // n_t, 0, 0)

    return cur, prev, cur_stream


def _whole(shape):
    return pl.BlockSpec(shape, lambda i, _n=len(shape): (0,) * _n)


def _layer0_scratch(ns, tm):
    m = ns * tm
    return [pltpu.VMEM((ns, HALO + tm + BF16_ROWS, D_MODEL), F32),
            pltpu.VMEM((m, D_MODEL), F32),
            pltpu.VMEM((m, D_MODEL), BF16),
            pltpu.VMEM((m, D_FF), BF16)]


def _conv_tables(tm):
    rb = min(tm, CONV_ROWS)
    return [pltpu.VMEM((CONV_W, BF16_ROWS, D_MODEL), BF16),
            pltpu.VMEM((rb, SUBLANES * (rb + BF16_ROWS)), BF16)]


def _layer0_prompt(x, state, w0, *, tm):
    nb, seq, _ = x.shape
    n_t = seq // tm
    tile = (1, tm, D_MODEL)
    tile_map = lambda i: (i // n_t, i % n_t, 0)
    stream_map = lambda i: (i // n_t, 0, 0)
    return pl.pallas_call(
        functools.partial(_layer0_prompt_kernel, tm=tm, n_t=n_t),
        grid=(nb * n_t,),
        in_specs=[pl.BlockSpec(tile, tile_map), pl.BlockSpec((1, HALO, D_MODEL), stream_map)]
                 + [_resident(a, l) for a, l in w0],
        out_specs=[pl.BlockSpec(tile, tile_map), pl.BlockSpec((1, HALO, D_MODEL), stream_map)],
        out_shape=[jax.ShapeDtypeStruct((nb, seq, D_MODEL), F32),
                   jax.ShapeDtypeStruct((nb, HALO, D_MODEL), F32)],
        scratch_shapes=_layer0_scratch(1, tm) + _conv_tables(tm),
        compiler_params=_params(),
        name="layer0_prompt",
    )(x, state, *[a for a, _ in w0])


def _layer0_sample(x, state, w0):
    ns, tm, _ = x.shape
    m = ns * tm
    return pl.pallas_call(
        functools.partial(_layer0_sample_kernel, ns=ns, tm=tm),
        grid=(1,),
        in_specs=[_whole((ns, tm, D_MODEL)), _whole((ns, HALO, D_MODEL))] + [_resident(a, l) for a, l in w0],
        out_specs=[_whole((ns, tm, D_MODEL)), _whole((ns, HALO, D_MODEL))],
        out_shape=[jax.ShapeDtypeStruct((ns, tm, D_MODEL), F32),
                   jax.ShapeDtypeStruct((ns, HALO, D_MODEL), F32)],
        scratch_shapes=_layer0_scratch(ns, tm) + [pltpu.VMEM((m, D_MODEL), F32)] + _conv_tables(tm),
        compiler_params=_params(),
        name="layer0_sample",
    )(x, state, *[a for a, _ in w0])


def _layer1_scratch(ns, tm, tq):
    m = ns * tm
    return [pltpu.VMEM((m, D_MODEL), BF16),
            pltpu.VMEM((ns, WINDOW + tm, KV_W), BF16),
            pltpu.VMEM((ns, WINDOW + tm, KV_W), BF16),
            pltpu.VMEM((m, D_MODEL), BF16),
            pltpu.VMEM((m, D_FF), BF16),
            pltpu.VMEM((N_KV, GROUP * tq, WINDOW + tq), F32),
            pltpu.VMEM((m, D_MODEL), F32)]


def _layer1_prompt(x, sinks, w1, *, tm):
    nb, seq, _ = x.shape
    n_t = seq // tm
    n_tiles = nb * n_t
    cur, prev, cur_stream = _tile_maps(n_t, n_tiles)
    tile = (1, tm, D_MODEL)
    cache = pl.BlockSpec((1, WINDOW, KV_W), cur_stream)
    return pl.pallas_call(
        functools.partial(_layer1_prompt_kernel, tm=tm, n_t=n_t, n_tiles=n_tiles),
        grid=(n_tiles + 1,),
        in_specs=[pl.BlockSpec(tile, cur), pl.BlockSpec(memory_space=pltpu.SMEM)]
                 + [_resident(a, l) for a, l in w1],
        out_specs=[pl.BlockSpec(tile, prev), cache, cache],
        out_shape=[jax.ShapeDtypeStruct((nb, seq, D_MODEL), F32),
                   jax.ShapeDtypeStruct((nb, WINDOW, KV_W), F32),
                   jax.ShapeDtypeStruct((nb, WINDOW, KV_W), F32)],
        scratch_shapes=_layer1_scratch(1, tm, CHUNK),
        compiler_params=_params(),
        name="layer1_prompt",
    )(x, sinks, *[a for a, _ in w1])


def _layer1_sample(x, kc, vc, sinks, w1):
    ns, tm, _ = x.shape
    cache = _whole((ns, WINDOW, KV_W))
    return pl.pallas_call(
        functools.partial(_layer1_sample_kernel, ns=ns, tm=tm),
        grid=(1,),
        in_specs=[_whole((ns, tm, D_MODEL)), cache, cache, pl.BlockSpec(memory_space=pltpu.SMEM)]
                 + [_resident(a, l) for a, l in w1],
        out_specs=[_whole((ns, tm, D_MODEL)), cache, cache],
        out_shape=[jax.ShapeDtypeStruct((ns, tm, D_MODEL), F32),
                   jax.ShapeDtypeStruct((ns, WINDOW, KV_W), F32),
                   jax.ShapeDtypeStruct((ns, WINDOW, KV_W), F32)],
        scratch_shapes=_layer1_scratch(ns, tm, tm),
        compiler_params=_params(),
        name="layer1_sample",
    )(x, kc, vc, sinks, *[a for a, _ in w1])


def kernel(x_prompt, x_sample, state_conv, cache_k, cache_v, g_mix, g_ffn, w_pw1, b_pw1, w_dw, b_dw,
           ln_g, ln_b, w_pw2, b_pw2, w_qkv, qn_g, kn_g, sinks, w_o, w_gate, w_up, w_down):
    nb = x_prompt.shape[0]
    ndb = x_sample.shape[0]
    row = lambda a: (a.reshape(1, -1), None)
    wg, wu, wd = w_gate.astype(BF16), w_up.astype(BF16), w_down.astype(BF16)
    w0 = [row(g_mix[0]), (w_pw1.astype(BF16), 0), row(b_pw1[0]), (w_dw, 0), row(b_dw[0]), row(ln_g[0]),
          row(ln_b[0]), (w_pw2.astype(BF16), 0), row(b_pw2[0]), row(g_ffn[0]), (wg, 0), (wu, 0), (wd, 0)]
    w1 = [row(g_mix[1]), (w_qkv.astype(BF16), 0), row(jnp.tile(qn_g[0], N_HEADS)), row(jnp.tile(kn_g[0], N_KV)),
          (w_o.astype(BF16), 0), row(g_ffn[1]), (wg, 1), (wu, 1), (wd, 1)]

    zero_state = jnp.zeros((nb, HALO, D_MODEL), F32)
    sample_state = jnp.pad(state_conv[0], ((0, 0), (HALO_PAD, 0), (0, 0)))
    xp1, cst_p = _layer0_prompt(x_prompt, zero_state, w0, tm=PROMPT_TM)
    xs1, cst_s = _layer0_sample(x_sample, sample_state, w0)

    kc = cache_k[0].reshape(ndb, WINDOW, KV_W)
    vc = cache_v[0].reshape(ndb, WINDOW, KV_W)
    yp, kp, vp = _layer1_prompt(xp1, sinks[0], w1, tm=PROMPT_TM)
    ys, ks, vs = _layer1_sample(xs1, kc, vc, sinks[0], w1)

    kv_shape = lambda a: a.reshape(1, a.shape[0], WINDOW, N_KV, HEAD_DIM)
    return (yp, ys, cst_p[None, :, HALO_PAD:, :], cst_s[None, :, HALO_PAD:, :],
            kv_shape(kp), kv_shape(vp), kv_shape(ks), kv_shape(vs))
```

```python
import functools

import jax
import jax.numpy as jnp
from jax import lax
from jax.experimental import pallas as pl
from jax.experimental.pallas import tpu as pltpu

D_MODEL = 1024
CHUNK = 64
CONV_W = 31
HEAD_DIM = 64
N_HEADS = D_MODEL // HEAD_DIM
N_KV = 4
GROUP = N_HEADS // N_KV
WINDOW = 128
D_FF = 2816
KV_W = N_KV * HEAD_DIM
SCALE = HEAD_DIM ** -0.5
NEG = -1e30
RMS_EPS = 1e-6
LN_EPS = 1e-5

SUBLANES = 8
HALO = -(-(CONV_W - 1) // SUBLANES) * SUBLANES
HALO_PAD = HALO - (CONV_W - 1)
MXU_N = 256
CONV_ROWS = 128
CONV_COLS = 256
BF16_ROWS = 16
LN_ROWS = 64
VMEM_LIMIT = 56 * 1024 * 1024
PROMPT_TM = 512

F32 = jnp.float32
BF16 = jnp.bfloat16


def _dot(a, b):
    return jnp.dot(a, b, preferred_element_type=F32)


def _rms(x, g):
    ms = jnp.mean(x * x, axis=-1, keepdims=True)
    return x * lax.rsqrt(ms + RMS_EPS) * g


def _run(gen):
    for _ in gen:
        pass


def _interleave(main, n_main, side, n_side):
    next(side)
    done = 1
    for i in range(n_main):
        next(main)
        want = max(1, min(n_side, -(-(n_side * (i + 1)) // (n_main - 1))))
        for _ in range(want - done):
            next(side)
        done = want
    _run(main)
    _run(side)


FFN_ITEMS = D_FF // MXU_N + D_MODEL // MXU_N


def _ffn_items(x1, gf_ref, wg_ref, wu_ref, wd_ref, act_ref, y_ref):
    hn = _rms(x1, gf_ref[...]).astype(BF16)
    for j in range(D_FF // MXU_N):
        sl = slice(j * MXU_N, (j + 1) * MXU_N)
        gt = _dot(hn, wg_ref[:, sl])
        up = _dot(hn, wu_ref[:, sl])
        act_ref[:, sl] = (gt * jax.nn.sigmoid(gt) * up).astype(BF16)
        yield
    for j in range(D_MODEL // MXU_N):
        sl = slice(j * MXU_N, (j + 1) * MXU_N)
        dn = _dot(act_ref[...], wd_ref[:, sl])
        y_ref[:, sl] = x1[:, sl] + dn
        yield


def _init_conv_tables(wdw_ref, wb_ref, shift_ref):
    for k in range(CONV_W):
        wb_ref[k] = jnp.broadcast_to(wdw_ref[k:k + 1, :], (BF16_ROWS, D_MODEL)).astype(BF16)
    rb, cols = shift_ref.shape
    pr = cols // SUBLANES
    r = lax.broadcasted_iota(jnp.int32, (rb, cols), 0)
    c = lax.broadcasted_iota(jnp.int32, (rb, cols), 1)
    shift_ref[...] = jnp.where(c % pr == r + c // pr, 1.0, 0.0).astype(BF16)


def _l0_mixer(x, w, gext_ref, cbuf_ref, sw_ref, wb_ref, shift_ref, *, ns, tm, carry):
    (gm_ref, wpw1_ref, bpw1_ref, bdw_ref, lng_ref, lnb_ref) = w
    m = ns * tm
    h = _rms(x, gm_ref[...]).astype(BF16)
    a = _dot(h, wpw1_ref[:, :D_MODEL]) + bpw1_ref[:, :D_MODEL]
    gate = _dot(h, wpw1_ref[:, D_MODEL:]) + bpw1_ref[:, D_MODEL:]
    g = a * jax.nn.sigmoid(gate)
    gext_ref[:, HALO:HALO + tm, :] = g.reshape(ns, tm, D_MODEL)

    rb = min(tm, CONV_ROWS)
    pr = rb + BF16_ROWS
    nz = rb + HALO + BF16_ROWS
    for s in range(ns):
        for r0 in range(0, tm, rb):
            for c0 in range(0, D_MODEL, CONV_COLS):
                cs = slice(c0, c0 + CONV_COLS)
                z = gext_ref[s, pl.ds(r0, nz), cs]
                zb = (z.astype(BF16), z[SUBLANES:nz - SUBLANES].astype(BF16))
                parts = []
                for p in range(SUBLANES):
                    part = None
                    for k in range(CONV_W):
                        if (HALO_PAD + k) % SUBLANES != p:
                            continue
                        odd = ((HALO_PAD + k - p) // SUBLANES) % 2
                        off = HALO_PAD + k - p - SUBLANES * odd
                        wt = jnp.concatenate([wb_ref[k, :, cs]] * (pr // BF16_ROWS), axis=0)
                        term = zb[odd][off:off + pr] * wt
                        part = term if part is None else part + term
                    parts.append(part)
                acc = _dot(shift_ref[...], jnp.concatenate(parts, axis=0))
                cbuf_ref[pl.ds(s * tm + r0, rb), cs] = acc + bdw_ref[:, cs]
    if carry:
        gext_ref[:, :HALO, :] = gext_ref[:, tm:tm + HALO, :]

    lr = min(m, LN_ROWS)
    for r0 in range(0, m, lr):
        c = cbuf_ref[pl.ds(r0, lr), :]
        mu = jnp.mean(c, axis=-1, keepdims=True)
        xc = c - mu
        ln = xc * lax.rsqrt(jnp.mean(xc * xc, axis=-1, keepdims=True) + LN_EPS) * lng_ref[...] + lnb_ref[...]
        sw_ref[pl.ds(r0, lr), :] = (ln * jax.nn.sigmoid(ln)).astype(BF16)


def _l0_ffn(xres, sw_ref, w, act_ref, y_ref):
    (wpw2_ref, bpw2_ref, gf_ref, wg_ref, wu_ref, wd_ref) = w
    x1 = xres + _dot(sw_ref[...], wpw2_ref[...]) + bpw2_ref[...]
    _run(_ffn_items(x1, gf_ref, wg_ref, wu_ref, wd_ref, act_ref, y_ref))


def _layer0_prompt_kernel(x_ref, st_ref, gm_ref, wpw1_ref, bpw1_ref, wdw_ref, bdw_ref,
                          lng_ref, lnb_ref, wpw2_ref, bpw2_ref, gf_ref, wg_ref, wu_ref, wd_ref,
                          y_ref, nst_ref, gext_ref, cbuf_ref, sw_ref, act_ref, wb_ref, shift_ref, *, tm, n_t):
    i = pl.program_id(0)
    t = lax.rem(i, n_t)

    @pl.when(i == 0)
    def _():
        _init_conv_tables(wdw_ref, wb_ref, shift_ref)
        gext_ref[:, HALO + tm:, :] = jnp.zeros((1, BF16_ROWS, D_MODEL), F32)

    @pl.when(t == 0)
    def _():
        gext_ref[:, :HALO, :] = st_ref[...]

    x = x_ref[0]
    _l0_mixer(x, (gm_ref, wpw1_ref, bpw1_ref, bdw_ref, lng_ref, lnb_ref), gext_ref, cbuf_ref, sw_ref, wb_ref,
              shift_ref, ns=1, tm=tm, carry=True)
    _l0_ffn(x, sw_ref, (wpw2_ref, bpw2_ref, gf_ref, wg_ref, wu_ref, wd_ref), act_ref, y_ref.at[0])

    @pl.when(t == n_t - 1)
    def _():
        nst_ref[...] = gext_ref[:, tm:tm + HALO, :]


def _layer0_sample_kernel(x_ref, st_ref, gm_ref, wpw1_ref, bpw1_ref, wdw_ref, bdw_ref, lng_ref, lnb_ref,
                          wpw2_ref, bpw2_ref, gf_ref, wg_ref, wu_ref, wd_ref,
                          y_ref, nst_ref, gext_ref, cbuf_ref, sw_ref, act_ref, yflat_ref, wb_ref, shift_ref,
                          *, ns, tm):
    m = ns * tm
    _init_conv_tables(wdw_ref, wb_ref, shift_ref)
    gext_ref[:, HALO + tm:, :] = jnp.zeros((ns, BF16_ROWS, D_MODEL), F32)
    gext_ref[:, :HALO, :] = st_ref[...]
    x = x_ref[...].reshape(m, D_MODEL)
    _l0_mixer(x, (gm_ref, wpw1_ref, bpw1_ref, bdw_ref, lng_ref, lnb_ref), gext_ref, cbuf_ref, sw_ref, wb_ref,
              shift_ref, ns=ns, tm=tm, carry=False)
    nst_ref[...] = gext_ref[:, tm:tm + HALO, :]
    _l0_ffn(x, sw_ref, (wpw2_ref, bpw2_ref, gf_ref, wg_ref, wu_ref, wd_ref), act_ref, yflat_ref)
    y_ref[...] = yflat_ref[...].reshape(ns, tm, D_MODEL)


def _head_mean_square(z):
    r = lax.broadcasted_iota(jnp.int32, (MXU_N, MXU_N), 0) // HEAD_DIM
    c = lax.broadcasted_iota(jnp.int32, (MXU_N, MXU_N), 1) // HEAD_DIM
    ones_bd = jnp.where(r == c, 1.0, 0.0).astype(BF16)
    outs = []
    for j in range(z.shape[1] // MXU_N):
        zz = z[:, j * MXU_N:(j + 1) * MXU_N]
        outs.append(_dot((zz * zz).astype(BF16), ones_bd))
    ss = outs[0] if len(outs) == 1 else jnp.concatenate(outs, axis=1)
    return ss * (1.0 / HEAD_DIM)


def _init_alibi_bias(bias_ref, tq):
    rows4, nk = bias_ref.shape[1:]
    rows = lax.broadcasted_iota(jnp.int32, (rows4, nk), 0)
    cols = lax.broadcasted_iota(jnp.int32, (rows4, nk), 1)
    dist = jnp.abs((rows % tq) + WINDOW - cols).astype(F32)
    grp = rows // tq
    for kvh in range(N_KV):
        slope = jnp.zeros((rows4, nk), F32)
        for g_ in range(GROUP):
            slope = jnp.where(grp == g_, 2.0 ** (-8.0 * (kvh * GROUP + g_ + 1) / N_HEADS), slope)
        bias_ref[kvh] = -(slope * dist)


def _l1_mixer_items(x, w, caches, sinks_ref, ko_ref, vo_ref, qn_ref, kbuf_ref, vbuf_ref, ao_ref, bias_ref,
                    *, ns, tm, tq, first_pos, carry):
    (gm_ref, wqkv_ref, qg_ref, kg_ref) = w
    nk = WINDOW + tq
    rows4 = GROUP * tq
    h = _rms(x, gm_ref[...]).astype(BF16)
    q = _dot(h, wqkv_ref[:, :D_MODEL])
    k = _dot(h, wqkv_ref[:, D_MODEL:D_MODEL + KV_W])
    v = _dot(h, wqkv_ref[:, D_MODEL + KV_W:])
    qn = q * lax.rsqrt(_head_mean_square(q) + RMS_EPS) * (qg_ref[...] * SCALE)
    kn = k * lax.rsqrt(_head_mean_square(k) + RMS_EPS) * kg_ref[...]
    qn_ref[...] = qn.astype(BF16)
    kbuf_ref[:, WINDOW:, :] = kn.astype(BF16).reshape(ns, tm, KV_W)
    vbuf_ref[:, WINDOW:, :] = v.astype(BF16).reshape(ns, tm, KV_W)
    kn3 = kn.reshape(ns, tm, KV_W)
    v3 = v.reshape(ns, tm, KV_W)
    if tm >= WINDOW:
        ko_ref[...] = kn3[:, tm - WINDOW:, :]
        vo_ref[...] = v3[:, tm - WINDOW:, :]
    else:
        kc_ref, vc_ref = caches
        ko_ref[:, :WINDOW - tm, :] = kc_ref[:, tm:, :]
        vo_ref[:, :WINDOW - tm, :] = vc_ref[:, tm:, :]
        ko_ref[:, WINDOW - tm:, :] = kn3
        vo_ref[:, WINDOW - tm:, :] = v3
    yield

    grp_col = lax.broadcasted_iota(jnp.int32, (rows4, 1), 0) // tq
    sink_cols = []
    for kvh in range(N_KV):
        sc = jnp.zeros((rows4, 1), F32)
        for g_ in range(GROUP):
            sc = jnp.where(grp_col == g_, sinks_ref[kvh * GROUP + g_], sc)
        sink_cols.append(sc)
    key_col = lax.broadcasted_iota(jnp.int32, (rows4, nk), 1)

    for s in range(ns):
        for c in range(tm // tq):
            r0 = c * tq
            row0 = s * tm + r0
            for kvh in range(N_KV):
                q4 = jnp.concatenate(
                    [qn_ref[pl.ds(row0, tq), (kvh * GROUP + g_) * HEAD_DIM:(kvh * GROUP + g_ + 1) * HEAD_DIM]
                     for g_ in range(GROUP)], axis=0)
                ks = slice(kvh * HEAD_DIM, (kvh + 1) * HEAD_DIM)
                kb = kbuf_ref[s, pl.ds(r0, nk), ks]
                vb = vbuf_ref[s, pl.ds(r0, nk), ks]
                sc = lax.dot_general(q4, kb, (((1,), (1,)), ((), ())), preferred_element_type=F32)
                sc = sc + bias_ref[kvh]
                if first_pos is not None:
                    sc = jnp.where(key_col >= WINDOW - r0 - first_pos, sc, NEG)
                sink = sink_cols[kvh]
                mx = jnp.maximum(jnp.max(sc, axis=-1, keepdims=True), sink)
                p = jnp.exp(sc - mx)
                denom = jnp.sum(p, axis=-1, keepdims=True) + jnp.exp(sink - mx)
                o = _dot(p.astype(BF16), vb) / denom
                o_cat = jnp.concatenate([o[g_ * tq:(g_ + 1) * tq] for g_ in range(GROUP)], axis=1)
                ao_ref[pl.ds(row0, tq), kvh * GROUP * HEAD_DIM:(kvh + 1) * GROUP * HEAD_DIM] = o_cat.astype(BF16)
                yield
    if carry:
        kbuf_ref[:, :WINDOW, :] = kbuf_ref[:, tm:, :]
        vbuf_ref[:, :WINDOW, :] = vbuf_ref[:, tm:, :]


def _l1_mixer_count(ns, tm, tq):
    return 1 + ns * (tm // tq) * N_KV


def _l1_ffn_items(xres, ao_ref, w, act_ref, y_ref):
    (wo_ref, gf_ref, wg_ref, wu_ref, wd_ref) = w
    x1 = xres + _dot(ao_ref[...], wo_ref[...])
    yield
    yield from _ffn_items(x1, gf_ref, wg_ref, wu_ref, wd_ref, act_ref, y_ref)


def _l1_ffn_then_proj_items(x, x1_ref, ao_ref, w, act_ref, y_ref, *, do_ffn=True, do_proj=True):
    (wo_ref, gf_ref, wg_ref, wu_ref, wd_ref) = w
    if do_ffn:
        yield from _ffn_items(x1_ref[...], gf_ref, wg_ref, wu_ref, wd_ref, act_ref, y_ref)
    if do_proj:
        x1 = x + _dot(ao_ref[...], wo_ref[...])
        x1_ref[...] = x1
        yield


def _layer1_prompt_kernel(x_ref, sinks_ref, gm_ref, wqkv_ref, qg_ref, kg_ref, wo_ref, gf_ref,
                          wg_ref, wu_ref, wd_ref, y_ref, ko_ref, vo_ref,
                          qn_ref, kbuf_ref, vbuf_ref, ao_ref, act_ref, bias_ref, x1_ref, *, tm, n_t, n_tiles):
    i = pl.program_id(0)
    t = lax.rem(jnp.minimum(i, n_tiles - 1), n_t)

    @pl.when(i == 0)
    def _():
        _init_alibi_bias(bias_ref, CHUNK)

    @pl.when(t == 0)
    def _():
        kbuf_ref[:, :WINDOW, :] = jnp.zeros((1, WINDOW, KV_W), BF16)
        vbuf_ref[:, :WINDOW, :] = jnp.zeros((1, WINDOW, KV_W), BF16)

    def ffn(**stages):
        return _l1_ffn_then_proj_items(x_ref[0], x1_ref, ao_ref, (wo_ref, gf_ref, wg_ref, wu_ref, wd_ref),
                                       act_ref, y_ref.at[0], **stages)

    def mixer():
        return _l1_mixer_items(x_ref[0], (gm_ref, wqkv_ref, qg_ref, kg_ref), None, sinks_ref, ko_ref, vo_ref,
                               qn_ref, kbuf_ref, vbuf_ref, ao_ref, bias_ref,
                               ns=1, tm=tm, tq=CHUNK, first_pos=t * tm, carry=True)

    @pl.when(i == 0)
    def _():
        _run(mixer())
        _run(ffn(do_ffn=False))

    @pl.when((i > 0) & (i < n_tiles))
    def _():
        _interleave(ffn(), FFN_ITEMS + 1, mixer(), _l1_mixer_count(1, tm, CHUNK))

    @pl.when(i == n_tiles)
    def _():
        _run(ffn(do_proj=False))


def _layer1_sample_kernel(x_ref, kc_ref, vc_ref, sinks_ref, gm_ref, wqkv_ref, qg_ref, kg_ref, wo_ref, gf_ref,
                          wg_ref, wu_ref, wd_ref, y_ref, ko_ref, vo_ref,
                          qn_ref, kbuf_ref, vbuf_ref, ao_ref, act_ref, bias_ref, yflat_ref, *, ns, tm):
    m = ns * tm
    _init_alibi_bias(bias_ref, tm)
    kbuf_ref[:, :WINDOW, :] = kc_ref[...].astype(BF16)
    vbuf_ref[:, :WINDOW, :] = vc_ref[...].astype(BF16)
    x = x_ref[...].reshape(m, D_MODEL)
    _run(_l1_mixer_items(x, (gm_ref, wqkv_ref, qg_ref, kg_ref), (kc_ref, vc_ref), sinks_ref, ko_ref, vo_ref,
                         qn_ref, kbuf_ref, vbuf_ref, ao_ref, bias_ref,
                         ns=ns, tm=tm, tq=tm, first_pos=None, carry=False))
    _run(_l1_ffn_items(x, ao_ref, (wo_ref, gf_ref, wg_ref, wu_ref, wd_ref), act_ref, yflat_ref))
    y_ref[...] = yflat_ref[...].reshape(ns, tm, D_MODEL)


def _resident(arr, layer=None):
    if layer is None:
        nd = arr.ndim
        return pl.BlockSpec(arr.shape, lambda *_, _nd=nd: (0,) * _nd, pipeline_mode=pl.Buffered(1))
    nd = arr.ndim - 1
    return pl.BlockSpec((None,) + arr.shape[1:], lambda *_, _nd=nd, _l=layer: (_l,) + (0,) * _nd,
                        pipeline_mode=pl.Buffered(1))


def _params():
    return pltpu.CompilerParams(dimension_semantics=("arbitrary",), vmem_limit_bytes=VMEM_LIMIT)


def _tile_maps(n_t, n_tiles):
    def cur(i):
        j = jnp.minimum(i, n_tiles - 1)
        return (j // n_t, j % n_t, 0)

    def prev(i):
        j = jnp.maximum(i - 1, 0)
        return (j // n_t, j % n_t, 0)

    def cur_stream(i):
        return (jnp.minimum(i, n_tiles - 1) // n_t, 0, 0)

    return cur, prev, cur_stream


def _whole(shape):
    return pl.BlockSpec(shape, lambda i, _n=len(shape): (0,) * _n)


def _layer0_scratch(ns, tm):
    m = ns * tm
    return [pltpu.VMEM((ns, HALO + tm + BF16_ROWS, D_MODEL), F32),
            pltpu.VMEM((m, D_MODEL), F32),
            pltpu.VMEM((m, D_MODEL), BF16),
            pltpu.VMEM((m, D_FF), BF16)]


def _conv_tables(tm):
    rb = min(tm, CONV_ROWS)
    return [pltpu.VMEM((CONV_W, BF16_ROWS, D_MODEL), BF16),
            pltpu.VMEM((rb, SUBLANES * (rb + BF16_ROWS)), BF16)]


def _layer0_prompt(x, state, w0, *, tm):
    nb, seq, _ = x.shape
    n_t = seq // tm
    tile = (1, tm, D_MODEL)
    tile_map = lambda i: (i // n_t, i % n_t, 0)
    stream_map = lambda i: (i // n_t, 0, 0)
    return pl.pallas_call(
        functools.partial(_layer0_prompt_kernel, tm=tm, n_t=n_t),
        grid=(nb * n_t,),
        in_specs=[pl.BlockSpec(tile, tile_map), pl.BlockSpec((1, HALO, D_MODEL), stream_map)]
                 + [_resident(a, l) for a, l in w0],
        out_specs=[pl.BlockSpec(tile, tile_map), pl.BlockSpec((1, HALO, D_MODEL), stream_map)],
        out_shape=[jax.ShapeDtypeStruct((nb, seq, D_MODEL), F32),
                   jax.ShapeDtypeStruct((nb, HALO, D_MODEL), F32)],
        scratch_shapes=_layer0_scratch(1, tm) + _conv_tables(tm),
        compiler_params=_params(),
        name="layer0_prompt",
    )(x, state, *[a for a, _ in w0])


def _layer0_sample(x, state, w0):
    ns, tm, _ = x.shape
    m = ns * tm
    return pl.pallas_call(
        functools.partial(_layer0_sample_kernel, ns=ns, tm=tm),
        grid=(1,),
        in_specs=[_whole((ns, tm, D_MODEL)), _whole((ns, HALO, D_MODEL))] + [_resident(a, l) for a, l in w0],
        out_specs=[_whole((ns, tm, D_MODEL)), _whole((ns, HALO, D_MODEL))],
        out_shape=[jax.ShapeDtypeStruct((ns, tm, D_MODEL), F32),
                   jax.ShapeDtypeStruct((ns, HALO, D_MODEL), F32)],
        scratch_shapes=_layer0_scratch(ns, tm) + [pltpu.VMEM((m, D_MODEL), F32)] + _conv_tables(tm),
        compiler_params=_params(),
        name="layer0_sample",
    )(x, state, *[a for a, _ in w0])


def _layer1_scratch(ns, tm, tq):
    m = ns * tm
    return [pltpu.VMEM((m, D_MODEL), BF16),
            pltpu.VMEM((ns, WINDOW + tm, KV_W), BF16),
            pltpu.VMEM((ns, WINDOW + tm, KV_W), BF16),
            pltpu.VMEM((m, D_MODEL), BF16),
            pltpu.VMEM((m, D_FF), BF16),
            pltpu.VMEM((N_KV, GROUP * tq, WINDOW + tq), F32),
            pltpu.VMEM((m, D_MODEL), F32)]


def _layer1_prompt(x, sinks, w1, *, tm):
    nb, seq, _ = x.shape
    n_t = seq // tm
    n_tiles = nb * n_t
    cur, prev, cur_stream = _tile_maps(n_t, n_tiles)
    tile = (1, tm, D_MODEL)
    cache = pl.BlockSpec((1, WINDOW, KV_W), cur_stream)
    return pl.pallas_call(
        functools.partial(_layer1_prompt_kernel, tm=tm, n_t=n_t, n_tiles=n_tiles),
        grid=(n_tiles + 1,),
        in_specs=[pl.BlockSpec(tile, cur), pl.BlockSpec(memory_space=pltpu.SMEM)]
                 + [_resident(a, l) for a, l in w1],
        out_specs=[pl.BlockSpec(tile, prev), cache, cache],
        out_shape=[jax.ShapeDtypeStruct((nb, seq, D_MODEL), F32),
                   jax.ShapeDtypeStruct((nb, WINDOW, KV_W), F32),
                   jax.ShapeDtypeStruct((nb, WINDOW, KV_W), F32)],
        scratch_shapes=_layer1_scratch(1, tm, CHUNK),
        compiler_params=_params(),
        name="layer1_prompt",
    )(x, sinks, *[a for a, _ in w1])


def _layer1_sample(x, kc, vc, sinks, w1):
    ns, tm, _ = x.shape
    cache = _whole((ns, WINDOW, KV_W))
    return pl.pallas_call(
        functools.partial(_layer1_sample_kernel, ns=ns, tm=tm),
        grid=(1,),
        in_specs=[_whole((ns, tm, D_MODEL)), cache, cache, pl.BlockSpec(memory_space=pltpu.SMEM)]
                 + [_resident(a, l) for a, l in w1],
        out_specs=[_whole((ns, tm, D_MODEL)), cache, cache],
        out_shape=[jax.ShapeDtypeStruct((ns, tm, D_MODEL), F32),
                   jax.ShapeDtypeStruct((ns, WINDOW, KV_W), F32),
                   jax.ShapeDtypeStruct((ns, WINDOW, KV_W), F32)],
        scratch_shapes=_layer1_scratch(ns, tm, tm),
        compiler_params=_params(),
        name="layer1_sample",
    )(x, kc, vc, sinks, *[a for a, _ in w1])


def kernel(x_prompt, x_sample, state_conv, cache_k, cache_v, g_mix, g_ffn, w_pw1, b_pw1, w_dw, b_dw,
           ln_g, ln_b, w_pw2, b_pw2, w_qkv, qn_g, kn_g, sinks, w_o, w_gate, w_up, w_down):
    nb = x_prompt.shape[0]
    ndb = x_sample.shape[0]
    row = lambda a: (a.reshape(1, -1), None)
    wg, wu, wd = w_gate.astype(BF16), w_up.astype(BF16), w_down.astype(BF16)
    w0 = [row(g_mix[0]), (w_pw1.astype(BF16), 0), row(b_pw1[0]), (w_dw, 0), row(b_dw[0]), row(ln_g[0]),
          row(ln_b[0]), (w_pw2.astype(BF16), 0), row(b_pw2[0]), row(g_ffn[0]), (wg, 0), (wu, 0), (wd, 0)]
    w1 = [row(g_mix[1]), (w_qkv.astype(BF16), 0), row(jnp.tile(qn_g[0], N_HEADS)), row(jnp.tile(kn_g[0], N_KV)),
          (w_o.astype(BF16), 0), row(g_ffn[1]), (wg, 1), (wu, 1), (wd, 1)]

    zero_state = jnp.zeros((nb, HALO, D_MODEL), F32)
    sample_state = jnp.pad(state_conv[0], ((0, 0), (HALO_PAD, 0), (0, 0)))
    xp1, cst_p = _layer0_prompt(x_prompt, zero_state, w0, tm=PROMPT_TM)
    xs1, cst_s = _layer0_sample(x_sample, sample_state, w0)

    kc = cache_k[0].reshape(ndb, WINDOW, KV_W)
    vc = cache_v[0].reshape(ndb, WINDOW, KV_W)
    yp, kp, vp = _layer1_prompt(xp1, sinks[0], w1, tm=PROMPT_TM)
    ys, ks, vs = _layer1_sample(xs1, kc, vc, sinks[0], w1)

    kv_shape = lambda a: a.reshape(1, a.shape[0], WINDOW, N_KV, HEAD_DIM)
    return (yp, ys, cst_p[None, :, HALO_PAD:, :], cst_s[None, :, HALO_PAD:, :],
            kv_shape(kp), kv_shape(vp), kv_shape(ks), kv_shape(vs))
```

```python
import functools

import jax
import jax.numpy as jnp
from jax import lax
from jax.experimental import pallas as pl
from jax.experimental.pallas import tpu as pltpu

D_MODEL = 1024
CHUNK = 64
CONV_W = 31
HEAD_DIM = 64
N_HEADS = D_MODEL // HEAD_DIM
N_KV = 4
GROUP = N_HEADS // N_KV
WINDOW = 128
D_FF = 2816
KV_W = N_KV * HEAD_DIM
SCALE = HEAD_DIM ** -0.5
NEG = -1e30
RMS_EPS = 1e-6
LN_EPS = 1e-5

SUBLANES = 8
HALO = -(-(CONV_W - 1) // SUBLANES) * SUBLANES
HALO_PAD = HALO - (CONV_W - 1)
MXU_N = 256
CONV_ROWS = 128
CONV_COLS = 256
BF16_ROWS = 16
LN_ROWS = 64
PW1_ROWS = 256
VMEM_LIMIT = 56 * 1024 * 1024
PROMPT_TM = 512

F32 = jnp.float32
BF16 = jnp.bfloat16


def _dot(a, b):
    return jnp.dot(a, b, preferred_element_type=F32)


def _rms(x, g):
    ms = jnp.mean(x * x, axis=-1, keepdims=True)
    return x * lax.rsqrt(ms + RMS_EPS) * g


def _run(gen):
    for _ in gen:
        pass


def _interleave(main, n_main, side, n_side):
    next(side)
    done = 1
    for i in range(n_main):
        next(main)
        want = max(1, min(n_side, -(-(n_side * (i + 1)) // (n_main - 1))))
        for _ in range(want - done):
            next(side)
        done = want
    _run(main)
    _run(side)


FFN_ITEMS = D_FF // MXU_N + D_MODEL // MXU_N


def _ffn_items(x1, gf_ref, wg_ref, wu_ref, wd_ref, act_ref, y_ref):
    hn = _rms(x1, gf_ref[...]).astype(BF16)
    for j in range(D_FF // MXU_N):
        sl = slice(j * MXU_N, (j + 1) * MXU_N)
        gt = _dot(hn, wg_ref[:, sl])
        up = _dot(hn, wu_ref[:, sl])
        act_ref[:, sl] = (gt * jax.nn.sigmoid(gt) * up).astype(BF16)
        yield
    for j in range(D_MODEL // MXU_N):
        sl = slice(j * MXU_N, (j + 1) * MXU_N)
        dn = _dot(act_ref[...], wd_ref[:, sl])
        y_ref[:, sl] = x1[:, sl] + dn
        yield


def _init_conv_tables(wdw_ref, wb_ref, shift_ref):
    for k in range(CONV_W):
        wb_ref[k] = jnp.broadcast_to(wdw_ref[k:k + 1, :], (BF16_ROWS, D_MODEL)).astype(BF16)
    rb, cols = shift_ref.shape
    pr = cols // SUBLANES
    r = lax.broadcasted_iota(jnp.int32, (rb, cols), 0)
    c = lax.broadcasted_iota(jnp.int32, (rb, cols), 1)
    shift_ref[...] = jnp.where(c % pr == r + c // pr, 1.0, 0.0).astype(BF16)


def _l0_mixer(x, w, gext_ref, cbuf_ref, sw_ref, wb_ref, shift_ref, *, ns, tm, carry):
    (gm_ref, wpw1_ref, bpw1_ref, bdw_ref, lng_ref, lnb_ref) = w
    m = ns * tm
    pb = min(tm, PW1_ROWS) if ns == 1 else m
    for q0 in range(0, m, pb):
        h = _rms(x[q0:q0 + pb], gm_ref[...]).astype(BF16)
        a = _dot(h, wpw1_ref[:, :D_MODEL]) + bpw1_ref[:, :D_MODEL]
        gate = _dot(h, wpw1_ref[:, D_MODEL:]) + bpw1_ref[:, D_MODEL:]
        g = a * jax.nn.sigmoid(gate)
        if ns == 1:
            gext_ref[0, HALO + q0:HALO + q0 + pb, :] = g
        else:
            gext_ref[:, HALO:HALO + tm, :] = g.reshape(ns, tm, D_MODEL)

    rb = min(tm, CONV_ROWS)
    pr = rb + BF16_ROWS
    nz = rb + HALO + BF16_ROWS
    for s in range(ns):
        for r0 in range(0, tm, rb):
            for c0 in range(0, D_MODEL, CONV_COLS):
                cs = slice(c0, c0 + CONV_COLS)
                z = gext_ref[s, pl.ds(r0, nz), cs]
                zb = (z.astype(BF16), z[SUBLANES:nz - SUBLANES].astype(BF16))
                parts = []
                for p in range(SUBLANES):
                    part = None
                    for k in range(CONV_W):
                        if (HALO_PAD + k) % SUBLANES != p:
                            continue
                        odd = ((HALO_PAD + k - p) // SUBLANES) % 2
                        off = HALO_PAD + k - p - SUBLANES * odd
                        wt = jnp.concatenate([wb_ref[k, :, cs]] * (pr // BF16_ROWS), axis=0)
                        term = zb[odd][off:off + pr] * wt
                        part = term if part is None else part + term
                    parts.append(part)
                acc = _dot(shift_ref[...], jnp.concatenate(parts, axis=0))
                cbuf_ref[pl.ds(s * tm + r0, rb), cs] = acc + bdw_ref[:, cs]
    if carry:
        gext_ref[:, :HALO, :] = gext_ref[:, tm:tm + HALO, :]

    lr = min(m, LN_ROWS)
    for r0 in range(0, m, lr):
        c = cbuf_ref[pl.ds(r0, lr), :]
        mu = jnp.mean(c, axis=-1, keepdims=True)
        xc = c - mu
        ln = xc * lax.rsqrt(jnp.mean(xc * xc, axis=-1, keepdims=True) + LN_EPS) * lng_ref[...] + lnb_ref[...]
        sw_ref[pl.ds(r0, lr), :] = (ln * jax.nn.sigmoid(ln)).astype(BF16)


def _l0_ffn(xres, sw_ref, w, act_ref, y_ref):
    (wpw2_ref, bpw2_ref, gf_ref, wg_ref, wu_ref, wd_ref) = w
    x1 = xres + _dot(sw_ref[...], wpw2_ref[...]) + bpw2_ref[...]
    _run(_ffn_items(x1, gf_ref, wg_ref, wu_ref, wd_ref, act_ref, y_ref))


def _layer0_prompt_kernel(x_ref, st_ref, gm_ref, wpw1_ref, bpw1_ref, wdw_ref, bdw_ref,
                          lng_ref, lnb_ref, wpw2_ref, bpw2_ref, gf_ref, wg_ref, wu_ref, wd_ref,
                          y_ref, nst_ref, gext_ref, cbuf_ref, sw_ref, act_ref, wb_ref, shift_ref, *, tm, n_t):
    i = pl.program_id(0)
    t = lax.rem(i, n_t)

    @pl.when(i == 0)
    def _():
        _init_conv_tables(wdw_ref, wb_ref, shift_ref)
        gext_ref[:, HALO + tm:, :] = jnp.zeros((1, BF16_ROWS, D_MODEL), F32)

    @pl.when(t == 0)
    def _():
        gext_ref[:, :HALO, :] = st_ref[...]

    x = x_ref[0]
    _l0_mixer(x, (gm_ref, wpw1_ref, bpw1_ref, bdw_ref, lng_ref, lnb_ref), gext_ref, cbuf_ref, sw_ref, wb_ref,
              shift_ref, ns=1, tm=tm, carry=True)
    _l0_ffn(x, sw_ref, (wpw2_ref, bpw2_ref, gf_ref, wg_ref, wu_ref, wd_ref), act_ref, y_ref.at[0])

    @pl.when(t == n_t - 1)
    def _():
        nst_ref[...] = gext_ref[:, tm:tm + HALO, :]


def _layer0_sample_kernel(x_ref, st_ref, gm_ref, wpw1_ref, bpw1_ref, wdw_ref, bdw_ref, lng_ref, lnb_ref,
                          wpw2_ref, bpw2_ref, gf_ref, wg_ref, wu_ref, wd_ref,
                          y_ref, nst_ref, gext_ref, cbuf_ref, sw_ref, act_ref, yflat_ref, wb_ref, shift_ref,
                          *, ns, tm):
    m = ns * tm
    _init_conv_tables(wdw_ref, wb_ref, shift_ref)
    gext_ref[:, HALO + tm:, :] = jnp.zeros((ns, BF16_ROWS, D_MODEL), F32)
    gext_ref[:, :HALO, :] = st_ref[...]
    x = x_ref[...].reshape(m, D_MODEL)
    _l0_mixer(x, (gm_ref, wpw1_ref, bpw1_ref, bdw_ref, lng_ref, lnb_ref), gext_ref, cbuf_ref, sw_ref, wb_ref,
              shift_ref, ns=ns, tm=tm, carry=False)
    nst_ref[...] = gext_ref[:, tm:tm + HALO, :]
    _l0_ffn(x, sw_ref, (wpw2_ref, bpw2_ref, gf_ref, wg_ref, wu_ref, wd_ref), act_ref, yflat_ref)
    y_ref[...] = yflat_ref[...].reshape(ns, tm, D_MODEL)


def _head_mean_square(z):
    r = lax.broadcasted_iota(jnp.int32, (MXU_N, MXU_N), 0) // HEAD_DIM
    c = lax.broadcasted_iota(jnp.int32, (MXU_N, MXU_N), 1) // HEAD_DIM
    ones_bd = jnp.where(r == c, 1.0, 0.0).astype(BF16)
    outs = []
    for j in range(z.shape[1] // MXU_N):
        zz = z[:, j * MXU_N:(j + 1) * MXU_N]
        outs.append(_dot((zz * zz).astype(BF16), ones_bd))
    ss = outs[0] if len(outs) == 1 else jnp.concatenate(outs, axis=1)
    return ss * (1.0 / HEAD_DIM)


def _init_alibi_bias(bias_ref, tq):
    rows4, nk = bias_ref.shape[1:]
    rows = lax.broadcasted_iota(jnp.int32, (rows4, nk), 0)
    cols = lax.broadcasted_iota(jnp.int32, (rows4, nk), 1)
    dist = jnp.abs((rows % tq) + WINDOW - cols).astype(F32)
    grp = rows // tq
    for kvh in range(N_KV):
        slope = jnp.zeros((rows4, nk), F32)
        for g_ in range(GROUP):
            slope = jnp.where(grp == g_, 2.0 ** (-8.0 * (kvh * GROUP + g_ + 1) / N_HEADS), slope)
        bias_ref[kvh] = -(slope * dist)


def _l1_mixer_items(x, w, caches, sinks_ref, ko_ref, vo_ref, qn_ref, kbuf_ref, vbuf_ref, ao_ref, bias_ref,
                    *, ns, tm, tq, first_pos, carry):
    (gm_ref, wqkv_ref, qg_ref, kg_ref) = w
    nk = WINDOW + tq
    rows4 = GROUP * tq
    h = _rms(x, gm_ref[...]).astype(BF16)
    q = _dot(h, wqkv_ref[:, :D_MODEL])
    k = _dot(h, wqkv_ref[:, D_MODEL:D_MODEL + KV_W])
    v = _dot(h, wqkv_ref[:, D_MODEL + KV_W:])
    qn = q * lax.rsqrt(_head_mean_square(q) + RMS_EPS) * (qg_ref[...] * SCALE)
    kn = k * lax.rsqrt(_head_mean_square(k) + RMS_EPS) * kg_ref[...]
    qn_ref[...] = qn.astype(BF16)
    kbuf_ref[:, WINDOW:, :] = kn.astype(BF16).reshape(ns, tm, KV_W)
    vbuf_ref[:, WINDOW:, :] = v.astype(BF16).reshape(ns, tm, KV_W)
    kn3 = kn.reshape(ns, tm, KV_W)
    v3 = v.reshape(ns, tm, KV_W)
    if tm >= WINDOW:
        ko_ref[...] = kn3[:, tm - WINDOW:, :]
        vo_ref[...] = v3[:, tm - WINDOW:, :]
    else:
        kc_ref, vc_ref = caches
        ko_ref[:, :WINDOW - tm, :] = kc_ref[:, tm:, :]
        vo_ref[:, :WINDOW - tm, :] = vc_ref[:, tm:, :]
        ko_ref[:, WINDOW - tm:, :] = kn3
        vo_ref[:, WINDOW - tm:, :] = v3
    yield

    grp_col = lax.broadcasted_iota(jnp.int32, (rows4, 1), 0) // tq
    sink_cols = []
    for kvh in range(N_KV):
        sc = jnp.zeros((rows4, 1), F32)
        for g_ in range(GROUP):
            sc = jnp.where(grp_col == g_, sinks_ref[kvh * GROUP + g_], sc)
        sink_cols.append(sc)
    key_col = lax.broadcasted_iota(jnp.int32, (rows4, nk), 1)

    for s in range(ns):
        for c in range(tm // tq):
            r0 = c * tq
            row0 = s * tm + r0
            for kvh in range(N_KV):
                q4 = jnp.concatenate(
                    [qn_ref[pl.ds(row0, tq), (kvh * GROUP + g_) * HEAD_DIM:(kvh * GROUP + g_ + 1) * HEAD_DIM]
                     for g_ in range(GROUP)], axis=0)
                ks = slice(kvh * HEAD_DIM, (kvh + 1) * HEAD_DIM)
                kb = kbuf_ref[s, pl.ds(r0, nk), ks]
                vb = vbuf_ref[s, pl.ds(r0, nk), ks]
                sc = lax.dot_general(q4, kb, (((1,), (1,)), ((), ())), preferred_element_type=F32)
                sc = sc + bias_ref[kvh]
                if first_pos is not None:
                    sc = jnp.where(key_col >= WINDOW - r0 - first_pos, sc, NEG)
                sink = sink_cols[kvh]
                mx = jnp.maximum(jnp.max(sc, axis=-1, keepdims=True), sink)
                p = jnp.exp(sc - mx)
                denom = jnp.sum(p, axis=-1, keepdims=True) + jnp.exp(sink - mx)
                o = _dot(p.astype(BF16), vb) / denom
                o_cat = jnp.concatenate([o[g_ * tq:(g_ + 1) * tq] for g_ in range(GROUP)], axis=1)
                ao_ref[pl.ds(row0, tq), kvh * GROUP * HEAD_DIM:(kvh + 1) * GROUP * HEAD_DIM] = o_cat.astype(BF16)
                yield
    if carry:
        kbuf_ref[:, :WINDOW, :] = kbuf_ref[:, tm:, :]
        vbuf_ref[:, :WINDOW, :] = vbuf_ref[:, tm:, :]


def _l1_mixer_count(ns, tm, tq):
    return 1 + ns * (tm // tq) * N_KV


def _l1_ffn_items(xres, ao_ref, w, act_ref, y_ref):
    (wo_ref, gf_ref, wg_ref, wu_ref, wd_ref) = w
    x1 = xres + _dot(ao_ref[...], wo_ref[...])
    yield
    yield from _ffn_items(x1, gf_ref, wg_ref, wu_ref, wd_ref, act_ref, y_ref)


def _l1_ffn_then_proj_items(x, x1_ref, ao_ref, w, act_ref, y_ref, *, do_ffn=True, do_proj=True):
    (wo_ref, gf_ref, wg_ref, wu_ref, wd_ref) = w
    if do_ffn:
        yield from _ffn_items(x1_ref[...], gf_ref, wg_ref, wu_ref, wd_ref, act_ref, y_ref)
    if do_proj:
        x1 = x + _dot(ao_ref[...], wo_ref[...])
        x1_ref[...] = x1
        yield


def _layer1_prompt_kernel(x_ref, sinks_ref, gm_ref, wqkv_ref, qg_ref, kg_ref, wo_ref, gf_ref,
                          wg_ref, wu_ref, wd_ref, y_ref, ko_ref, vo_ref,
                          qn_ref, kbuf_ref, vbuf_ref, ao_ref, act_ref, bias_ref, x1_ref, *, tm, n_t, n_tiles):
    i = pl.program_id(0)
    t = lax.rem(jnp.minimum(i, n_tiles - 1), n_t)

    @pl.when(i == 0)
    def _():
        _init_alibi_bias(bias_ref, CHUNK)

    @pl.when(t == 0)
    def _():
        kbuf_ref[:, :WINDOW, :] = jnp.zeros((1, WINDOW, KV_W), BF16)
        vbuf_ref[:, :WINDOW, :] = jnp.zeros((1, WINDOW, KV_W), BF16)

    def ffn(**stages):
        return _l1_ffn_then_proj_items(x_ref[0], x1_ref, ao_ref, (wo_ref, gf_ref, wg_ref, wu_ref, wd_ref),
                                       act_ref, y_ref.at[0], **stages)

    def mixer():
        return _l1_mixer_items(x_ref[0], (gm_ref, wqkv_ref, qg_ref, kg_ref), None, sinks_ref, ko_ref, vo_ref,
                               qn_ref, kbuf_ref, vbuf_ref, ao_ref, bias_ref,
                               ns=1, tm=tm, tq=CHUNK, first_pos=t * tm, carry=True)

    @pl.when(i == 0)
    def _():
        _run(mixer())
        _run(ffn(do_ffn=False))

    @pl.when((i > 0) & (i < n_tiles))
    def _():
        _interleave(ffn(), FFN_ITEMS + 1, mixer(), _l1_mixer_count(1, tm, CHUNK))

    @pl.when(i == n_tiles)
    def _():
        _run(ffn(do_proj=False))


def _layer1_sample_kernel(x_ref, kc_ref, vc_ref, sinks_ref, gm_ref, wqkv_ref, qg_ref, kg_ref, wo_ref, gf_ref,
                          wg_ref, wu_ref, wd_ref, y_ref, ko_ref, vo_ref,
                          qn_ref, kbuf_ref, vbuf_ref, ao_ref, act_ref, bias_ref, yflat_ref, *, ns, tm):
    m = ns * tm
    _init_alibi_bias(bias_ref, tm)
    kbuf_ref[:, :WINDOW, :] = kc_ref[...].astype(BF16)
    vbuf_ref[:, :WINDOW, :] = vc_ref[...].astype(BF16)
    x = x_ref[...].reshape(m, D_MODEL)
    _run(_l1_mixer_items(x, (gm_ref, wqkv_ref, qg_ref, kg_ref), (kc_ref, vc_ref), sinks_ref, ko_ref, vo_ref,
                         qn_ref, kbuf_ref, vbuf_ref, ao_ref, bias_ref,
                         ns=ns, tm=tm, tq=tm, first_pos=None, carry=False))
    _run(_l1_ffn_items(x, ao_ref, (wo_ref, gf_ref, wg_ref, wu_ref, wd_ref), act_ref, yflat_ref))
    y_ref[...] = yflat_ref[...].reshape(ns, tm, D_MODEL)


def _resident(arr, layer=None):
    if layer is None:
        nd = arr.ndim
        return pl.BlockSpec(arr.shape, lambda *_, _nd=nd: (0,) * _nd, pipeline_mode=pl.Buffered(1))
    nd = arr.ndim - 1
    return pl.BlockSpec((None,) + arr.shape[1:], lambda *_, _nd=nd, _l=layer: (_l,) + (0,) * _nd,
                        pipeline_mode=pl.Buffered(1))


def _params():
    return pltpu.CompilerParams(dimension_semantics=("arbitrary",), vmem_limit_bytes=VMEM_LIMIT)


def _tile_maps(n_t, n_tiles):
    def cur(i):
        j = jnp.minimum(i, n_tiles - 1)
        return (j // n_t, j % n_t, 0)

    def prev(i):
        j = jnp.maximum(i - 1, 0)
        return (j // n_t, j % n_t, 0)

    def cur_stream(i):
        return (jnp.minimum(i, n_tiles - 1) // n_t, 0, 0)

    return cur, prev, cur_stream


def _whole(shape):
    return pl.BlockSpec(shape, lambda i, _n=len(shape): (0,) * _n)


def _layer0_scratch(ns, tm):
    m = ns * tm
    return [pltpu.VMEM((ns, HALO + tm + BF16_ROWS, D_MODEL), F32),
            pltpu.VMEM((m, D_MODEL), F32),
            pltpu.VMEM((m, D_MODEL), BF16),
            pltpu.VMEM((m, D_FF), BF16)]


def _conv_tables(tm):
    rb = min(tm, CONV_ROWS)
    return [pltpu.VMEM((CONV_W, BF16_ROWS, D_MODEL), BF16),
            pltpu.VMEM((rb, SUBLANES * (rb + BF16_ROWS)), BF16)]


def _layer0_prompt(x, state, w0, *, tm):
    nb, seq, _ = x.shape
    n_t = seq // tm
    tile = (1, tm, D_MODEL)
    tile_map = lambda i: (i // n_t, i % n_t, 0)
    stream_map = lambda i: (i // n_t, 0, 0)
    return pl.pallas_call(
        functools.partial(_layer0_prompt_kernel, tm=tm, n_t=n_t),
        grid=(nb * n_t,),
        in_specs=[pl.BlockSpec(tile, tile_map), pl.BlockSpec((1, HALO, D_MODEL), stream_map)]
                 + [_resident(a, l) for a, l in w0],
        out_specs=[pl.BlockSpec(tile, tile_map), pl.BlockSpec((1, HALO, D_MODEL), stream_map)],
        out_shape=[jax.ShapeDtypeStruct((nb, seq, D_MODEL), F32),
                   jax.ShapeDtypeStruct((nb, HALO, D_MODEL), F32)],
        scratch_shapes=_layer0_scratch(1, tm) + _conv_tables(tm),
        compiler_params=_params(),
        name="layer0_prompt",
    )(x, state, *[a for a, _ in w0])


def _layer0_sample(x, state, w0):
    ns, tm, _ = x.shape
    m = ns * tm
    return pl.pallas_call(
        functools.partial(_layer0_sample_kernel, ns=ns, tm=tm),
        grid=(1,),
        in_specs=[_whole((ns, tm, D_MODEL)), _whole((ns, HALO, D_MODEL))] + [_resident(a, l) for a, l in w0],
        out_specs=[_whole((ns, tm, D_MODEL)), _whole((ns, HALO, D_MODEL))],
        out_shape=[jax.ShapeDtypeStruct((ns, tm, D_MODEL), F32),
                   jax.ShapeDtypeStruct((ns, HALO, D_MODEL), F32)],
        scratch_shapes=_layer0_scratch(ns, tm) + [pltpu.VMEM((m, D_MODEL), F32)] + _conv_tables(tm),
        compiler_params=_params(),
        name="layer0_sample",
    )(x, state, *[a for a, _ in w0])


def _layer1_scratch(ns, tm, tq):
    m = ns * tm
    return [pltpu.VMEM((m, D_MODEL), BF16),
            pltpu.VMEM((ns, WINDOW + tm, KV_W), BF16),
            pltpu.VMEM((ns, WINDOW + tm, KV_W), BF16),
            pltpu.VMEM((m, D_MODEL), BF16),
            pltpu.VMEM((m, D_FF), BF16),
            pltpu.VMEM((N_KV, GROUP * tq, WINDOW + tq), F32),
            pltpu.VMEM((m, D_MODEL), F32)]


def _layer1_prompt(x, sinks, w1, *, tm):
    nb, seq, _ = x.shape
    n_t = seq // tm
    n_tiles = nb * n_t
    cur, prev, cur_stream = _tile_maps(n_t, n_tiles)
    tile = (1, tm, D_MODEL)
    cache = pl.BlockSpec((1, WINDOW, KV_W), cur_stream)
    return pl.pallas_call(
        functools.partial(_layer1_prompt_kernel, tm=tm, n_t=n_t, n_tiles=n_tiles),
        grid=(n_tiles + 1,),
        in_specs=[pl.BlockSpec(tile, cur), pl.BlockSpec(memory_space=pltpu.SMEM)]
                 + [_resident(a, l) for a, l in w1],
        out_specs=[pl.BlockSpec(tile, prev), cache, cache],
        out_shape=[jax.ShapeDtypeStruct((nb, seq, D_MODEL), F32),
                   jax.ShapeDtypeStruct((nb, WINDOW, KV_W), F32),
                   jax.ShapeDtypeStruct((nb, WINDOW, KV_W), F32)],
        scratch_shapes=_layer1_scratch(1, tm, CHUNK),
        compiler_params=_params(),
        name="layer1_prompt",
    )(x, sinks, *[a for a, _ in w1])


def _layer1_sample(x, kc, vc, sinks, w1):
    ns, tm, _ = x.shape
    cache = _whole((ns, WINDOW, KV_W))
    return pl.pallas_call(
        functools.partial(_layer1_sample_kernel, ns=ns, tm=tm),
        grid=(1,),
        in_specs=[_whole((ns, tm, D_MODEL)), cache, cache, pl.BlockSpec(memory_space=pltpu.SMEM)]
                 + [_resident(a, l) for a, l in w1],
        out_specs=[_whole((ns, tm, D_MODEL)), cache, cache],
        out_shape=[jax.ShapeDtypeStruct((ns, tm, D_MODEL), F32),
                   jax.ShapeDtypeStruct((ns, WINDOW, KV_W), F32),
                   jax.ShapeDtypeStruct((ns, WINDOW, KV_W), F32)],
        scratch_shapes=_layer1_scratch(ns, tm, tm),
        compiler_params=_params(),
        name="layer1_sample",
    )(x, kc, vc, sinks, *[a for a, _ in w1])


def kernel(x_prompt, x_sample, state_conv, cache_k, cache_v, g_mix, g_ffn, w_pw1, b_pw1, w_dw, b_dw,
           ln_g, ln_b, w_pw2, b_pw2, w_qkv, qn_g, kn_g, sinks, w_o, w_gate, w_up, w_down):
    nb = x_prompt.shape[0]
    ndb = x_sample.shape[0]
    row = lambda a: (a.reshape(1, -1), None)
    wg, wu, wd = w_gate.astype(BF16), w_up.astype(BF16), w_down.astype(BF16)
    w0 = [row(g_mix[0]), (w_pw1.astype(BF16), 0), row(b_pw1[0]), (w_dw, 0), row(b_dw[0]), row(ln_g[0]),
          row(ln_b[0]), (w_pw2.astype(BF16), 0), row(b_pw2[0]), row(g_ffn[0]), (wg, 0), (wu, 0), (wd, 0)]
    w1 = [row(g_mix[1]), (w_qkv.astype(BF16), 0), row(jnp.tile(qn_g[0], N_HEADS)), row(jnp.tile(kn_g[0], N_KV)),
          (w_o.astype(BF16), 0), row(g_ffn[1]), (wg, 1), (wu, 1), (wd, 1)]

    zero_state = jnp.zeros((nb, HALO, D_MODEL), F32)
    sample_state = jnp.pad(state_conv[0], ((0, 0), (HALO_PAD, 0), (0, 0)))
    xp1, cst_p = _layer0_prompt(x_prompt, zero_state, w0, tm=PROMPT_TM)
    xs1, cst_s = _layer0_sample(x_sample, sample_state, w0)

    kc = cache_k[0].reshape(ndb, WINDOW, KV_W)
    vc = cache_v[0].reshape(ndb, WINDOW, KV_W)
    yp, kp, vp = _layer1_prompt(xp1, sinks[0], w1, tm=PROMPT_TM)
    ys, ks, vs = _layer1_sample(xs1, kc, vc, sinks[0], w1)

    kv_shape = lambda a: a.reshape(1, a.shape[0], WINDOW, N_KV, HEAD_DIM)
    return (yp, ys, cst_p[None, :, HALO_PAD:, :], cst_s[None, :, HALO_PAD:, :],
            kv_shape(kp), kv_shape(vp), kv_shape(ks), kv_shape(vs))
```

```python
import functools

import jax
import jax.numpy as jnp
from jax import lax
from jax.experimental import pallas as pl
from jax.experimental.pallas import tpu as pltpu

D_MODEL = 1024
CHUNK = 64
CONV_W = 31
HEAD_DIM = 64
N_HEADS = D_MODEL // HEAD_DIM
N_KV = 4
GROUP = N_HEADS // N_KV
WINDOW = 128
D_FF = 2816
KV_W = N_KV * HEAD_DIM
SCALE = HEAD_DIM ** -0.5
NEG = -1e30
RMS_EPS = 1e-6
LN_EPS = 1e-5

SUBLANES = 8
HALO = -(-(CONV_W - 1) // SUBLANES) * SUBLANES
HALO_PAD = HALO - (CONV_W - 1)
MXU_N = 256
CONV_ROWS = 128
CONV_COLS = 256
BF16_ROWS = 16
LN_ROWS = 64
PW1_ROWS = 256
VMEM_LIMIT = 56 * 1024 * 1024
PROMPT_TM = 512

F32 = jnp.float32
BF16 = jnp.bfloat16


def _dot(a, b):
    return jnp.dot(a, b, preferred_element_type=F32)


def _rms(x, g):
    ms = jnp.mean(x * x, axis=-1, keepdims=True)
    return x * lax.rsqrt(ms + RMS_EPS) * g


def _run(gen):
    for _ in gen:
        pass


def _interleave(main, n_main, side, n_side):
    next(side)
    done = 1
    for i in range(n_main):
        next(main)
        want = max(1, min(n_side, -(-(n_side * (i + 1)) // (n_main - 1))))
        for _ in range(want - done):
            next(side)
        done = want
    _run(main)
    _run(side)


FFN_ITEMS = D_FF // MXU_N + D_MODEL // MXU_N


def _ffn_items(x1, gf_ref, wg_ref, wu_ref, wd_ref, act_ref, y_ref):
    hn = _rms(x1, gf_ref[...]).astype(BF16)
    for j in range(D_FF // MXU_N):
        sl = slice(j * MXU_N, (j + 1) * MXU_N)
        gt = _dot(hn, wg_ref[:, sl])
        up = _dot(hn, wu_ref[:, sl])
        act_ref[:, sl] = (gt * jax.nn.sigmoid(gt) * up).astype(BF16)
        yield
    for j in range(D_MODEL // MXU_N):
        sl = slice(j * MXU_N, (j + 1) * MXU_N)
        dn = _dot(act_ref[...], wd_ref[:, sl])
        y_ref[:, sl] = x1[:, sl] + dn
        yield


def _init_conv_tables(wdw_ref, wb_ref, shift_ref):
    for k in range(CONV_W):
        wb_ref[k] = jnp.broadcast_to(wdw_ref[k:k + 1, :], (BF16_ROWS, D_MODEL)).astype(BF16)
    rb, cols = shift_ref.shape
    pr = cols // SUBLANES
    r = lax.broadcasted_iota(jnp.int32, (rb, cols), 0)
    c = lax.broadcasted_iota(jnp.int32, (rb, cols), 1)
    shift_ref[...] = jnp.where(c % pr == r + c // pr, 1.0, 0.0).astype(BF16)


def _l0_mixer(x, w, gext_ref, cbuf_ref, sw_ref, wb_ref, shift_ref, *, ns, tm, carry):
    (gm_ref, wpw1_ref, bpw1_ref, bdw_ref, lng_ref, lnb_ref) = w
    m = ns * tm
    pb = min(tm, PW1_ROWS) if ns == 1 else m
    for q0 in range(0, m, pb):
        h = _rms(x[q0:q0 + pb], gm_ref[...]).astype(BF16)
        a = _dot(h, wpw1_ref[:, :D_MODEL]) + bpw1_ref[:, :D_MODEL]
        gate = _dot(h, wpw1_ref[:, D_MODEL:]) + bpw1_ref[:, D_MODEL:]
        g = a * jax.nn.sigmoid(gate)
        if ns == 1:
            gext_ref[0, HALO + q0:HALO + q0 + pb, :] = g
        else:
            gext_ref[:, HALO:HALO + tm, :] = g.reshape(ns, tm, D_MODEL)

    rb = min(tm, CONV_ROWS)
    pr = rb + BF16_ROWS
    nz = rb + HALO + BF16_ROWS
    for s in range(ns):
        for r0 in range(0, tm, rb):
            for c0 in range(0, D_MODEL, CONV_COLS):
                cs = slice(c0, c0 + CONV_COLS)
                z = gext_ref[s, pl.ds(r0, nz), cs]
                zb = (z.astype(BF16), z[SUBLANES:nz - SUBLANES].astype(BF16))
                parts = []
                for p in range(SUBLANES):
                    part = None
                    for k in range(CONV_W):
                        if (HALO_PAD + k) % SUBLANES != p:
                            continue
                        odd = ((HALO_PAD + k - p) // SUBLANES) % 2
                        off = HALO_PAD + k - p - SUBLANES * odd
                        wt = jnp.concatenate([wb_ref[k, :, cs]] * (pr // BF16_ROWS), axis=0)
                        term = zb[odd][off:off + pr] * wt
                        part = term if part is None else part + term
                    parts.append(part)
                acc = _dot(shift_ref[...], jnp.concatenate(parts, axis=0))
                cbuf_ref[pl.ds(s * tm + r0, rb), cs] = acc + bdw_ref[:, cs]
    if carry:
        gext_ref[:, :HALO, :] = gext_ref[:, tm:tm + HALO, :]

    lr = min(m, LN_ROWS)
    for r0 in range(0, m, lr):
        c = cbuf_ref[pl.ds(r0, lr), :]
        mu = jnp.mean(c, axis=-1, keepdims=True)
        xc = c - mu
        ln = xc * lax.rsqrt(jnp.mean(xc * xc, axis=-1, keepdims=True) + LN_EPS) * lng_ref[...] + lnb_ref[...]
        sw_ref[pl.ds(r0, lr), :] = (ln * jax.nn.sigmoid(ln)).astype(BF16)


def _l0_ffn(xres, sw_ref, w, act_ref, y_ref):
    (wpw2_ref, bpw2_ref, gf_ref, wg_ref, wu_ref, wd_ref) = w
    x1 = xres + _dot(sw_ref[...], wpw2_ref[...]) + bpw2_ref[...]
    _run(_ffn_items(x1, gf_ref, wg_ref, wu_ref, wd_ref, act_ref, y_ref))


def _layer0_prompt_kernel(x_ref, gm_ref, wpw1_ref, bpw1_ref, wdw_ref, bdw_ref,
                          lng_ref, lnb_ref, wpw2_ref, bpw2_ref, gf_ref, wg_ref, wu_ref, wd_ref,
                          y_ref, nst_ref, gext_ref, cbuf_ref, sw_ref, act_ref, wb_ref, shift_ref, *, tm, n_t):
    i = pl.program_id(0)
    t = lax.rem(i, n_t)

    @pl.when(i == 0)
    def _():
        _init_conv_tables(wdw_ref, wb_ref, shift_ref)
        gext_ref[:, HALO + tm:, :] = jnp.zeros((1, BF16_ROWS, D_MODEL), F32)

    @pl.when(t == 0)
    def _():
        gext_ref[:, :HALO, :] = jnp.zeros((1, HALO, D_MODEL), F32)

    x = x_ref[0]
    _l0_mixer(x, (gm_ref, wpw1_ref, bpw1_ref, bdw_ref, lng_ref, lnb_ref), gext_ref, cbuf_ref, sw_ref, wb_ref,
              shift_ref, ns=1, tm=tm, carry=True)
    _l0_ffn(x, sw_ref, (wpw2_ref, bpw2_ref, gf_ref, wg_ref, wu_ref, wd_ref), act_ref, y_ref.at[0])

    @pl.when(t == n_t - 1)
    def _():
        nst_ref[...] = gext_ref[:, tm + HALO_PAD:tm + HALO, :]


def _layer0_sample_kernel(x_ref, st_ref, gm_ref, wpw1_ref, bpw1_ref, wdw_ref, bdw_ref, lng_ref, lnb_ref,
                          wpw2_ref, bpw2_ref, gf_ref, wg_ref, wu_ref, wd_ref,
                          y_ref, nst_ref, gext_ref, cbuf_ref, sw_ref, act_ref, yflat_ref, wb_ref, shift_ref,
                          *, ns, tm):
    m = ns * tm
    _init_conv_tables(wdw_ref, wb_ref, shift_ref)
    gext_ref[:, HALO + tm:, :] = jnp.zeros((ns, BF16_ROWS, D_MODEL), F32)
    gext_ref[:, :HALO_PAD, :] = jnp.zeros((ns, HALO_PAD, D_MODEL), F32)
    gext_ref[:, HALO_PAD:HALO, :] = st_ref[...]
    x = x_ref[...].reshape(m, D_MODEL)
    _l0_mixer(x, (gm_ref, wpw1_ref, bpw1_ref, bdw_ref, lng_ref, lnb_ref), gext_ref, cbuf_ref, sw_ref, wb_ref,
              shift_ref, ns=ns, tm=tm, carry=False)
    nst_ref[...] = gext_ref[:, tm + HALO_PAD:tm + HALO, :]
    _l0_ffn(x, sw_ref, (wpw2_ref, bpw2_ref, gf_ref, wg_ref, wu_ref, wd_ref), act_ref, yflat_ref)
    y_ref[...] = yflat_ref[...].reshape(ns, tm, D_MODEL)


def _head_mean_square(z):
    r = lax.broadcasted_iota(jnp.int32, (MXU_N, MXU_N), 0) // HEAD_DIM
    c = lax.broadcasted_iota(jnp.int32, (MXU_N, MXU_N), 1) // HEAD_DIM
    ones_bd = jnp.where(r == c, 1.0, 0.0).astype(BF16)
    outs = []
    for j in range(z.shape[1] // MXU_N):
        zz = z[:, j * MXU_N:(j + 1) * MXU_N]
        outs.append(_dot((zz * zz).astype(BF16), ones_bd))
    ss = outs[0] if len(outs) == 1 else jnp.concatenate(outs, axis=1)
    return ss * (1.0 / HEAD_DIM)


def _init_alibi_bias(bias_ref, tq):
    rows4, nk = bias_ref.shape[1:]
    rows = lax.broadcasted_iota(jnp.int32, (rows4, nk), 0)
    cols = lax.broadcasted_iota(jnp.int32, (rows4, nk), 1)
    dist = jnp.abs((rows % tq) + WINDOW - cols).astype(F32)
    grp = rows // tq
    for kvh in range(N_KV):
        slope = jnp.zeros((rows4, nk), F32)
        for g_ in range(GROUP):
            slope = jnp.where(grp == g_, 2.0 ** (-8.0 * (kvh * GROUP + g_ + 1) / N_HEADS), slope)
        bias_ref[kvh] = -(slope * dist)


def _l1_mixer_items(x, w, caches, sinks_ref, ko_ref, vo_ref, qn_ref, kbuf_ref, vbuf_ref, ao_ref, bias_ref,
                    *, ns, tm, tq, first_pos, carry):
    (gm_ref, wqkv_ref, qg_ref, kg_ref) = w
    nk = WINDOW + tq
    rows4 = GROUP * tq
    h = _rms(x, gm_ref[...]).astype(BF16)
    q = _dot(h, wqkv_ref[:, :D_MODEL])
    k = _dot(h, wqkv_ref[:, D_MODEL:D_MODEL + KV_W])
    v = _dot(h, wqkv_ref[:, D_MODEL + KV_W:])
    qn = q * lax.rsqrt(_head_mean_square(q) + RMS_EPS) * (qg_ref[...] * SCALE)
    kn = k * lax.rsqrt(_head_mean_square(k) + RMS_EPS) * kg_ref[...]
    qn_ref[...] = qn.astype(BF16)
    kbuf_ref[:, WINDOW:, :] = kn.astype(BF16).reshape(ns, tm, KV_W)
    vbuf_ref[:, WINDOW:, :] = v.astype(BF16).reshape(ns, tm, KV_W)
    kn3 = kn.reshape(ns, tm, KV_W)
    v3 = v.reshape(ns, tm, KV_W)
    if tm >= WINDOW:
        ko_ref[...] = kn3[:, tm - WINDOW:, :]
        vo_ref[...] = v3[:, tm - WINDOW:, :]
    else:
        kc_ref, vc_ref = caches
        ko_ref[:, :WINDOW - tm, :] = kc_ref[:, tm:, :]
        vo_ref[:, :WINDOW - tm, :] = vc_ref[:, tm:, :]
        ko_ref[:, WINDOW - tm:, :] = kn3
        vo_ref[:, WINDOW - tm:, :] = v3
    yield

    grp_col = lax.broadcasted_iota(jnp.int32, (rows4, 1), 0) // tq
    sink_cols = []
    for kvh in range(N_KV):
        sc = jnp.zeros((rows4, 1), F32)
        for g_ in range(GROUP):
            sc = jnp.where(grp_col == g_, sinks_ref[kvh * GROUP + g_], sc)
        sink_cols.append(sc)
    key_col = lax.broadcasted_iota(jnp.int32, (rows4, nk), 1)

    for s in range(ns):
        for c in range(tm // tq):
            r0 = c * tq
            row0 = s * tm + r0
            for kvh in range(N_KV):
                q4 = jnp.concatenate(
                    [qn_ref[pl.ds(row0, tq), (kvh * GROUP + g_) * HEAD_DIM:(kvh * GROUP + g_ + 1) * HEAD_DIM]
                     for g_ in range(GROUP)], axis=0)
                ks = slice(kvh * HEAD_DIM, (kvh + 1) * HEAD_DIM)
                kb = kbuf_ref[s, pl.ds(r0, nk), ks]
                vb = vbuf_ref[s, pl.ds(r0, nk), ks]
                sc = lax.dot_general(q4, kb, (((1,), (1,)), ((), ())), preferred_element_type=F32)
                sc = sc + bias_ref[kvh]
                if first_pos is not None:
                    sc = jnp.where(key_col >= WINDOW - r0 - first_pos, sc, NEG)
                sink = sink_cols[kvh]
                mx = jnp.maximum(jnp.max(sc, axis=-1, keepdims=True), sink)
                p = jnp.exp(sc - mx)
                denom = jnp.sum(p, axis=-1, keepdims=True) + jnp.exp(sink - mx)
                o = _dot(p.astype(BF16), vb) / denom
                o_cat = jnp.concatenate([o[g_ * tq:(g_ + 1) * tq] for g_ in range(GROUP)], axis=1)
                ao_ref[pl.ds(row0, tq), kvh * GROUP * HEAD_DIM:(kvh + 1) * GROUP * HEAD_DIM] = o_cat.astype(BF16)
                yield
    if carry:
        kbuf_ref[:, :WINDOW, :] = kbuf_ref[:, tm:, :]
        vbuf_ref[:, :WINDOW, :] = vbuf_ref[:, tm:, :]


def _l1_mixer_count(ns, tm, tq):
    return 1 + ns * (tm // tq) * N_KV


def _l1_ffn_items(xres, ao_ref, w, act_ref, y_ref):
    (wo_ref, gf_ref, wg_ref, wu_ref, wd_ref) = w
    x1 = xres + _dot(ao_ref[...], wo_ref[...])
    yield
    yield from _ffn_items(x1, gf_ref, wg_ref, wu_ref, wd_ref, act_ref, y_ref)


def _l1_ffn_then_proj_items(x, x1_ref, ao_ref, w, act_ref, y_ref, *, do_ffn=True, do_proj=True):
    (wo_ref, gf_ref, wg_ref, wu_ref, wd_ref) = w
    if do_ffn:
        yield from _ffn_items(x1_ref[...], gf_ref, wg_ref, wu_ref, wd_ref, act_ref, y_ref)
    if do_proj:
        x1 = x + _dot(ao_ref[...], wo_ref[...])
        x1_ref[...] = x1
        yield


def _layer1_prompt_kernel(x_ref, sinks_ref, gm_ref, wqkv_ref, qg_ref, kg_ref, wo_ref, gf_ref,
                          wg_ref, wu_ref, wd_ref, y_ref, ko_ref, vo_ref,
                          qn_ref, kbuf_ref, vbuf_ref, ao_ref, act_ref, bias_ref, x1_ref, kf_ref, vf_ref,
                          *, tm, n_t, n_tiles):
    i = pl.program_id(0)
    t = lax.rem(jnp.minimum(i, n_tiles - 1), n_t)

    @pl.when(i == 0)
    def _():
        _init_alibi_bias(bias_ref, CHUNK)

    @pl.when(t == 0)
    def _():
        kbuf_ref[:, :WINDOW, :] = jnp.zeros((1, WINDOW, KV_W), BF16)
        vbuf_ref[:, :WINDOW, :] = jnp.zeros((1, WINDOW, KV_W), BF16)

    def ffn(**stages):
        return _l1_ffn_then_proj_items(x_ref[0], x1_ref, ao_ref, (wo_ref, gf_ref, wg_ref, wu_ref, wd_ref),
                                       act_ref, y_ref.at[0], **stages)

    def mixer():
        return _l1_mixer_items(x_ref[0], (gm_ref, wqkv_ref, qg_ref, kg_ref), None, sinks_ref, kf_ref, vf_ref,
                               qn_ref, kbuf_ref, vbuf_ref, ao_ref, bias_ref,
                               ns=1, tm=tm, tq=CHUNK, first_pos=t * tm, carry=True)

    @pl.when(i == 0)
    def _():
        _run(mixer())
        _run(ffn(do_ffn=False))

    @pl.when((i > 0) & (i < n_tiles))
    def _():
        _interleave(ffn(), FFN_ITEMS + 1, mixer(), _l1_mixer_count(1, tm, CHUNK))

    @pl.when(i == n_tiles)
    def _():
        _run(ffn(do_proj=False))

    @pl.when((t == n_t - 1) & (i < n_tiles))
    def _():
        for hh in range(N_KV):
            ko_ref[:, :, hh, :] = kf_ref[:, :, hh * HEAD_DIM:(hh + 1) * HEAD_DIM]
            vo_ref[:, :, hh, :] = vf_ref[:, :, hh * HEAD_DIM:(hh + 1) * HEAD_DIM]


def _layer1_sample_kernel(x_ref, kc_ref, vc_ref, sinks_ref, gm_ref, wqkv_ref, qg_ref, kg_ref, wo_ref, gf_ref,
                          wg_ref, wu_ref, wd_ref, y_ref, ko_ref, vo_ref,
                          qn_ref, kbuf_ref, vbuf_ref, ao_ref, act_ref, bias_ref, yflat_ref, *, ns, tm):
    m = ns * tm
    _init_alibi_bias(bias_ref, tm)
    kbuf_ref[:, :WINDOW, :] = kc_ref[...].astype(BF16)
    vbuf_ref[:, :WINDOW, :] = vc_ref[...].astype(BF16)
    x = x_ref[...].reshape(m, D_MODEL)
    _run(_l1_mixer_items(x, (gm_ref, wqkv_ref, qg_ref, kg_ref), (kc_ref, vc_ref), sinks_ref, ko_ref, vo_ref,
                         qn_ref, kbuf_ref, vbuf_ref, ao_ref, bias_ref,
                         ns=ns, tm=tm, tq=tm, first_pos=None, carry=False))
    _run(_l1_ffn_items(x, ao_ref, (wo_ref, gf_ref, wg_ref, wu_ref, wd_ref), act_ref, yflat_ref))
    y_ref[...] = yflat_ref[...].reshape(ns, tm, D_MODEL)


def _resident(arr, layer=None):
    if layer is None:
        nd = arr.ndim
        return pl.BlockSpec(arr.shape, lambda *_, _nd=nd: (0,) * _nd, pipeline_mode=pl.Buffered(1))
    nd = arr.ndim - 1
    return pl.BlockSpec((None,) + arr.shape[1:], lambda *_, _nd=nd, _l=layer: (_l,) + (0,) * _nd,
                        pipeline_mode=pl.Buffered(1))


def _params():
    return pltpu.CompilerParams(dimension_semantics=("arbitrary",), vmem_limit_bytes=VMEM_LIMIT)


def _tile_maps(n_t, n_tiles):
    def cur(i):
        j = jnp.minimum(i, n_tiles - 1)
        return (j // n_t, j % n_t, 0)

    def prev(i):
        j = jnp.maximum(i - 1, 0)
        return (j // n_t, j % n_t, 0)

    def cur_stream(i):
        return (jnp.minimum(i, n_tiles - 1) // n_t, 0, 0)

    return cur, prev, cur_stream


def _whole(shape):
    return pl.BlockSpec(shape, lambda i, _n=len(shape): (0,) * _n)


def _layer0_scratch(ns, tm):
    m = ns * tm
    return [pltpu.VMEM((ns, HALO + tm + BF16_ROWS, D_MODEL), F32),
            pltpu.VMEM((m, D_MODEL), F32),
            pltpu.VMEM((m, D_MODEL), BF16),
            pltpu.VMEM((m, D_FF), BF16)]


def _conv_tables(tm):
    rb = min(tm, CONV_ROWS)
    return [pltpu.VMEM((CONV_W, BF16_ROWS, D_MODEL), BF16),
            pltpu.VMEM((rb, SUBLANES * (rb + BF16_ROWS)), BF16)]


def _layer0_prompt(x, w0, *, tm):
    nb, seq, _ = x.shape
    n_t = seq // tm
    tile = (1, tm, D_MODEL)
    tile_map = lambda i: (i // n_t, i % n_t, 0)
    stream_map = lambda i: (i // n_t, 0, 0)
    return pl.pallas_call(
        functools.partial(_layer0_prompt_kernel, tm=tm, n_t=n_t),
        grid=(nb * n_t,),
        in_specs=[pl.BlockSpec(tile, tile_map)] + [_resident(a, l) for a, l in w0],
        out_specs=[pl.BlockSpec(tile, tile_map), pl.BlockSpec((1, CONV_W - 1, D_MODEL), stream_map)],
        out_shape=[jax.ShapeDtypeStruct((nb, seq, D_MODEL), F32),
                   jax.ShapeDtypeStruct((nb, CONV_W - 1, D_MODEL), F32)],
        scratch_shapes=_layer0_scratch(1, tm) + _conv_tables(tm),
        compiler_params=_params(),
        name="layer0_prompt",
    )(x, *[a for a, _ in w0])


def _layer0_sample(x, state, w0):
    ns, tm, _ = x.shape
    m = ns * tm
    return pl.pallas_call(
        functools.partial(_layer0_sample_kernel, ns=ns, tm=tm),
        grid=(1,),
        in_specs=[_whole((ns, tm, D_MODEL)), _whole((ns, CONV_W - 1, D_MODEL))] + [_resident(a, l) for a, l in w0],
        out_specs=[_whole((ns, tm, D_MODEL)), _whole((ns, CONV_W - 1, D_MODEL))],
        out_shape=[jax.ShapeDtypeStruct((ns, tm, D_MODEL), F32),
                   jax.ShapeDtypeStruct((ns, CONV_W - 1, D_MODEL), F32)],
        scratch_shapes=_layer0_scratch(ns, tm) + [pltpu.VMEM((m, D_MODEL), F32)] + _conv_tables(tm),
        compiler_params=_params(),
        name="layer0_sample",
    )(x, state, *[a for a, _ in w0])


def _layer1_scratch(ns, tm, tq):
    m = ns * tm
    return [pltpu.VMEM((m, D_MODEL), BF16),
            pltpu.VMEM((ns, WINDOW + tm, KV_W), BF16),
            pltpu.VMEM((ns, WINDOW + tm, KV_W), BF16),
            pltpu.VMEM((m, D_MODEL), BF16),
            pltpu.VMEM((m, D_FF), BF16),
            pltpu.VMEM((N_KV, GROUP * tq, WINDOW + tq), F32),
            pltpu.VMEM((m, D_MODEL), F32)]


def _layer1_prompt(x, sinks, w1, *, tm):
    nb, seq, _ = x.shape
    n_t = seq // tm
    n_tiles = nb * n_t
    cur, prev, cur_stream = _tile_maps(n_t, n_tiles)
    tile = (1, tm, D_MODEL)
    cache = pl.BlockSpec((1, WINDOW, N_KV, HEAD_DIM), lambda i: cur_stream(i) + (0,))
    return pl.pallas_call(
        functools.partial(_layer1_prompt_kernel, tm=tm, n_t=n_t, n_tiles=n_tiles),
        grid=(n_tiles + 1,),
        in_specs=[pl.BlockSpec(tile, cur), pl.BlockSpec(memory_space=pltpu.SMEM)]
                 + [_resident(a, l) for a, l in w1],
        out_specs=[pl.BlockSpec(tile, prev), cache, cache],
        out_shape=[jax.ShapeDtypeStruct((nb, seq, D_MODEL), F32),
                   jax.ShapeDtypeStruct((nb, WINDOW, N_KV, HEAD_DIM), F32),
                   jax.ShapeDtypeStruct((nb, WINDOW, N_KV, HEAD_DIM), F32)],
        scratch_shapes=_layer1_scratch(1, tm, CHUNK) + [pltpu.VMEM((1, WINDOW, KV_W), F32)] * 2,
        compiler_params=_params(),
        name="layer1_prompt",
    )(x, sinks, *[a for a, _ in w1])


def _layer1_sample(x, kc, vc, sinks, w1):
    ns, tm, _ = x.shape
    cache = _whole((ns, WINDOW, KV_W))
    return pl.pallas_call(
        functools.partial(_layer1_sample_kernel, ns=ns, tm=tm),
        grid=(1,),
        in_specs=[_whole((ns, tm, D_MODEL)), cache, cache, pl.BlockSpec(memory_space=pltpu.SMEM)]
                 + [_resident(a, l) for a, l in w1],
        out_specs=[_whole((ns, tm, D_MODEL)), cache, cache],
        out_shape=[jax.ShapeDtypeStruct((ns, tm, D_MODEL), F32),
                   jax.ShapeDtypeStruct((ns, WINDOW, KV_W), F32),
                   jax.ShapeDtypeStruct((ns, WINDOW, KV_W), F32)],
        scratch_shapes=_layer1_scratch(ns, tm, tm),
        compiler_params=_params(),
        name="layer1_sample",
    )(x, kc, vc, sinks, *[a for a, _ in w1])


def kernel(x_prompt, x_sample, state_conv, cache_k, cache_v, g_mix, g_ffn, w_pw1, b_pw1, w_dw, b_dw,
           ln_g, ln_b, w_pw2, b_pw2, w_qkv, qn_g, kn_g, sinks, w_o, w_gate, w_up, w_down):
    ndb = x_sample.shape[0]
    row = lambda a: (a.reshape(1, -1), None)
    wg, wu, wd = w_gate.astype(BF16), w_up.astype(BF16), w_down.astype(BF16)
    w0 = [row(g_mix[0]), (w_pw1.astype(BF16), 0), row(b_pw1[0]), (w_dw, 0), row(b_dw[0]), row(ln_g[0]),
          row(ln_b[0]), (w_pw2.astype(BF16), 0), row(b_pw2[0]), row(g_ffn[0]), (wg, 0), (wu, 0), (wd, 0)]
    w1 = [row(g_mix[1]), (w_qkv.astype(BF16), 0), row(jnp.tile(qn_g[0], N_HEADS)), row(jnp.tile(kn_g[0], N_KV)),
          (w_o.astype(BF16), 0), row(g_ffn[1]), (wg, 1), (wu, 1), (wd, 1)]

    xp1, cst_p = _layer0_prompt(x_prompt, w0, tm=PROMPT_TM)
    xs1, cst_s = _layer0_sample(x_sample, state_conv[0], w0)

    kc = cache_k[0].reshape(ndb, WINDOW, KV_W)
    vc = cache_v[0].reshape(ndb, WINDOW, KV_W)
    yp, kp, vp = _layer1_prompt(xp1, sinks[0], w1, tm=PROMPT_TM)
    ys, ks, vs = _layer1_sample(xs1, kc, vc, sinks[0], w1)

    kv_shape = lambda a: a.reshape(1, a.shape[0], WINDOW, N_KV, HEAD_DIM)
    return (yp, ys, cst_p[None], cst_s[None], kp[None], vp[None], kv_shape(ks), kv_shape(vs))
```

```python
import functools

import jax
import jax.numpy as jnp
from jax import lax
from jax.experimental import pallas as pl
from jax.experimental.pallas import tpu as pltpu

D_MODEL = 1024
CHUNK = 64
CONV_W = 31
HEAD_DIM = 64
N_HEADS = D_MODEL // HEAD_DIM
N_KV = 4
GROUP = N_HEADS // N_KV
WINDOW = 128
D_FF = 2816
KV_W = N_KV * HEAD_DIM
SCALE = HEAD_DIM ** -0.5
NEG = -1e30
RMS_EPS = 1e-6
LN_EPS = 1e-5

SUBLANES = 8
HALO = -(-(CONV_W - 1) // SUBLANES) * SUBLANES
HALO_PAD = HALO - (CONV_W - 1)
MXU_N = 256
CONV_ROWS = 128
CONV_COLS = 256
BF16_ROWS = 16
LN_ROWS = 64
PW1_ROWS = 256
VMEM_LIMIT = 56 * 1024 * 1024
PROMPT_TM = 512

F32 = jnp.float32
BF16 = jnp.bfloat16


def _dot(a, b):
    return jnp.dot(a, b, preferred_element_type=F32)


def _rms(x, g):
    ms = jnp.mean(x * x, axis=-1, keepdims=True)
    return x * lax.rsqrt(ms + RMS_EPS) * g


def _run(gen):
    for _ in gen:
        pass


def _interleave(main, n_main, side, n_side):
    next(side)
    done = 1
    for i in range(n_main):
        next(main)
        want = max(1, min(n_side, -(-(n_side * (i + 1)) // (n_main - 1))))
        for _ in range(want - done):
            next(side)
        done = want
    _run(main)
    _run(side)


FFN_ITEMS = D_FF // MXU_N + D_MODEL // MXU_N


def _ffn_items(x1, gf_ref, wg_ref, wu_ref, wd_ref, act_ref, y_ref):
    hn = _rms(x1, gf_ref[...]).astype(BF16)
    for j in range(D_FF // MXU_N):
        sl = slice(j * MXU_N, (j + 1) * MXU_N)
        gt = _dot(hn, wg_ref[:, sl])
        up = _dot(hn, wu_ref[:, sl])
        act_ref[:, sl] = (gt * jax.nn.sigmoid(gt) * up).astype(BF16)
        yield
    for j in range(D_MODEL // MXU_N):
        sl = slice(j * MXU_N, (j + 1) * MXU_N)
        dn = _dot(act_ref[...], wd_ref[:, sl])
        y_ref[:, sl] = x1[:, sl] + dn
        yield


def _init_conv_tables(wdw_ref, wb_ref, shift_ref):
    for k in range(CONV_W):
        wb_ref[k] = jnp.broadcast_to(wdw_ref[k:k + 1, :], (BF16_ROWS, D_MODEL)).astype(BF16)
    rb, cols = shift_ref.shape
    pr = cols // SUBLANES
    r = lax.broadcasted_iota(jnp.int32, (rb, cols), 0)
    c = lax.broadcasted_iota(jnp.int32, (rb, cols), 1)
    shift_ref[...] = jnp.where(c % pr == r + c // pr, 1.0, 0.0).astype(BF16)


def _l0_mixer(x, w, gext_ref, cbuf_ref, sw_ref, wb_ref, shift_ref, *, ns, tm, carry):
    (gm_ref, wpw1_ref, bpw1_ref, bdw_ref, lng_ref, lnb_ref) = w
    m = ns * tm
    pb = min(tm, PW1_ROWS) if ns == 1 else m
    for q0 in range(0, m, pb):
        h = _rms(x[q0:q0 + pb], gm_ref[...]).astype(BF16)
        a = _dot(h, wpw1_ref[:, :D_MODEL]) + bpw1_ref[:, :D_MODEL]
        gate = _dot(h, wpw1_ref[:, D_MODEL:]) + bpw1_ref[:, D_MODEL:]
        g = a * jax.nn.sigmoid(gate)
        if ns == 1:
            gext_ref[0, HALO + q0:HALO + q0 + pb, :] = g
        else:
            gext_ref[:, HALO:HALO + tm, :] = g.reshape(ns, tm, D_MODEL)

    rb = min(tm, CONV_ROWS)
    pr = rb + BF16_ROWS
    nz = rb + HALO + BF16_ROWS
    for s in range(ns):
        for r0 in range(0, tm, rb):
            for c0 in range(0, D_MODEL, CONV_COLS):
                cs = slice(c0, c0 + CONV_COLS)
                z = gext_ref[s, pl.ds(r0, nz), cs]
                zb = (z.astype(BF16), z[SUBLANES:nz - SUBLANES].astype(BF16))
                parts = []
                for p in range(SUBLANES):
                    part = None
                    for k in range(CONV_W):
                        if (HALO_PAD + k) % SUBLANES != p:
                            continue
                        odd = ((HALO_PAD + k - p) // SUBLANES) % 2
                        off = HALO_PAD + k - p - SUBLANES * odd
                        wt = jnp.concatenate([wb_ref[k, :, cs]] * (pr // BF16_ROWS), axis=0)
                        term = zb[odd][off:off + pr] * wt
                        part = term if part is None else part + term
                    parts.append(part)
                acc = _dot(shift_ref[...], jnp.concatenate(parts, axis=0))
                cbuf_ref[pl.ds(s * tm + r0, rb), cs] = acc + bdw_ref[:, cs]
    if carry:
        gext_ref[:, :HALO, :] = gext_ref[:, tm:tm + HALO, :]

    lr = min(m, LN_ROWS)
    for r0 in range(0, m, lr):
        c = cbuf_ref[pl.ds(r0, lr), :]
        mu = jnp.mean(c, axis=-1, keepdims=True)
        xc = c - mu
        ln = xc * lax.rsqrt(jnp.mean(xc * xc, axis=-1, keepdims=True) + LN_EPS) * lng_ref[...] + lnb_ref[...]
        sw_ref[pl.ds(r0, lr), :] = (ln * jax.nn.sigmoid(ln)).astype(BF16)


def _l0_ffn(xres, sw_ref, w, act_ref, y_ref):
    (wpw2_ref, bpw2_ref, gf_ref, wg_ref, wu_ref, wd_ref) = w
    x1 = xres + _dot(sw_ref[...], wpw2_ref[...]) + bpw2_ref[...]
    _run(_ffn_items(x1, gf_ref, wg_ref, wu_ref, wd_ref, act_ref, y_ref))


def _layer0_kernel(x_ref, st_ref, xs_ref, sts_ref, gm_ref, wpw1_ref, bpw1_ref, wdw_ref, bdw_ref,
                   lng_ref, lnb_ref, wpw2_ref, bpw2_ref, gf_ref, wg_ref, wu_ref, wd_ref,
                   y_ref, nst_ref, ys_ref, nsts_ref,
                   gext_ref, cbuf_ref, sw_ref, act_ref, wb_ref, shift_ref, gexts_ref, shifts_ref, yflat_ref,
                   *, tm, n_t, n_tiles):
    i = pl.program_id(0)
    t = lax.rem(i, n_t)
    ns, tms, _ = xs_ref.shape
    ms = ns * tms
    mix_w = (gm_ref, wpw1_ref, bpw1_ref, bdw_ref, lng_ref, lnb_ref)
    ffn_w = (wpw2_ref, bpw2_ref, gf_ref, wg_ref, wu_ref, wd_ref)

    @pl.when(i == 0)
    def _():
        _init_conv_tables(wdw_ref, wb_ref, shift_ref)
        _init_conv_tables(wdw_ref, wb_ref, shifts_ref)
        gext_ref[:, HALO + tm:, :] = jnp.zeros((1, BF16_ROWS, D_MODEL), F32)
        gexts_ref[:, HALO + tms:, :] = jnp.zeros((ns, BF16_ROWS, D_MODEL), F32)

    @pl.when((t == 0) & (i < n_tiles))
    def _():
        gext_ref[:, :HALO, :] = st_ref[...]

    @pl.when(i < n_tiles)
    def _():
        x = x_ref[0]
        _l0_mixer(x, mix_w, gext_ref, cbuf_ref, sw_ref, wb_ref, shift_ref, ns=1, tm=tm, carry=True)
        _l0_ffn(x, sw_ref, ffn_w, act_ref, y_ref.at[0])

    @pl.when((t == n_t - 1) & (i < n_tiles))
    def _():
        nst_ref[...] = gext_ref[:, tm:tm + HALO, :]

    @pl.when(i == n_tiles)
    def _():
        gexts_ref[:, :HALO, :] = sts_ref[...]
        x = xs_ref[...].reshape(ms, D_MODEL)
        _l0_mixer(x, mix_w, gexts_ref, cbuf_ref.at[:ms], sw_ref.at[:ms], wb_ref, shifts_ref, ns=ns, tm=tms,
                  carry=False)
        nsts_ref[...] = gexts_ref[:, tms:tms + HALO, :]
        _l0_ffn(x, sw_ref.at[:ms], ffn_w, act_ref.at[:ms], yflat_ref)
        ys_ref[...] = yflat_ref[...].reshape(ns, tms, D_MODEL)


def _head_mean_square(z):
    r = lax.broadcasted_iota(jnp.int32, (MXU_N, MXU_N), 0) // HEAD_DIM
    c = lax.broadcasted_iota(jnp.int32, (MXU_N, MXU_N), 1) // HEAD_DIM
    ones_bd = jnp.where(r == c, 1.0, 0.0).astype(BF16)
    outs = []
    for j in range(z.shape[1] // MXU_N):
        zz = z[:, j * MXU_N:(j + 1) * MXU_N]
        outs.append(_dot((zz * zz).astype(BF16), ones_bd))
    ss = outs[0] if len(outs) == 1 else jnp.concatenate(outs, axis=1)
    return ss * (1.0 / HEAD_DIM)


def _init_alibi_bias(bias_ref, tq):
    rows4, nk = bias_ref.shape[1:]
    rows = lax.broadcasted_iota(jnp.int32, (rows4, nk), 0)
    cols = lax.broadcasted_iota(jnp.int32, (rows4, nk), 1)
    dist = jnp.abs((rows % tq) + WINDOW - cols).astype(F32)
    grp = rows // tq
    for kvh in range(N_KV):
        slope = jnp.zeros((rows4, nk), F32)
        for g_ in range(GROUP):
            slope = jnp.where(grp == g_, 2.0 ** (-8.0 * (kvh * GROUP + g_ + 1) / N_HEADS), slope)
        bias_ref[kvh] = -(slope * dist)


def _l1_mixer_items(x, w, caches, sinks_ref, ko_ref, vo_ref, qn_ref, kbuf_ref, vbuf_ref, ao_ref, bias_ref,
                    *, ns, tm, tq, first_pos, carry):
    (gm_ref, wqkv_ref, qg_ref, kg_ref) = w
    nk = WINDOW + tq
    rows4 = GROUP * tq
    h = _rms(x, gm_ref[...]).astype(BF16)
    q = _dot(h, wqkv_ref[:, :D_MODEL])
    k = _dot(h, wqkv_ref[:, D_MODEL:D_MODEL + KV_W])
    v = _dot(h, wqkv_ref[:, D_MODEL + KV_W:])
    qn = q * lax.rsqrt(_head_mean_square(q) + RMS_EPS) * (qg_ref[...] * SCALE)
    kn = k * lax.rsqrt(_head_mean_square(k) + RMS_EPS) * kg_ref[...]
    qn_ref[...] = qn.astype(BF16)
    kbuf_ref[:, WINDOW:, :] = kn.astype(BF16).reshape(ns, tm, KV_W)
    vbuf_ref[:, WINDOW:, :] = v.astype(BF16).reshape(ns, tm, KV_W)
    kn3 = kn.reshape(ns, tm, KV_W)
    v3 = v.reshape(ns, tm, KV_W)
    if tm >= WINDOW:
        ko_ref[...] = kn3[:, tm - WINDOW:, :]
        vo_ref[...] = v3[:, tm - WINDOW:, :]
    else:
        kc_ref, vc_ref = caches
        ko_ref[:, :WINDOW - tm, :] = kc_ref[:, tm:, :]
        vo_ref[:, :WINDOW - tm, :] = vc_ref[:, tm:, :]
        ko_ref[:, WINDOW - tm:, :] = kn3
        vo_ref[:, WINDOW - tm:, :] = v3
    yield

    grp_col = lax.broadcasted_iota(jnp.int32, (rows4, 1), 0) // tq
    sink_cols = []
    for kvh in range(N_KV):
        sc = jnp.zeros((rows4, 1), F32)
        for g_ in range(GROUP):
            sc = jnp.where(grp_col == g_, sinks_ref[kvh * GROUP + g_], sc)
        sink_cols.append(sc)
    key_col = lax.broadcasted_iota(jnp.int32, (rows4, nk), 1)

    for s in range(ns):
        for c in range(tm // tq):
            r0 = c * tq
            row0 = s * tm + r0
            for kvh in range(N_KV):
                q4 = jnp.concatenate(
                    [qn_ref[pl.ds(row0, tq), (kvh * GROUP + g_) * HEAD_DIM:(kvh * GROUP + g_ + 1) * HEAD_DIM]
                     for g_ in range(GROUP)], axis=0)
                ks = slice(kvh * HEAD_DIM, (kvh + 1) * HEAD_DIM)
                kb = kbuf_ref[s, pl.ds(r0, nk), ks]
                vb = vbuf_ref[s, pl.ds(r0, nk), ks]
                sc = lax.dot_general(q4, kb, (((1,), (1,)), ((), ())), preferred_element_type=F32)
                sc = sc + bias_ref[kvh]
                if first_pos is not None:
                    sc = jnp.where(key_col >= WINDOW - r0 - first_pos, sc, NEG)
                sink = sink_cols[kvh]
                mx = jnp.maximum(jnp.max(sc, axis=-1, keepdims=True), sink)
                p = jnp.exp(sc - mx)
                denom = jnp.sum(p, axis=-1, keepdims=True) + jnp.exp(sink - mx)
                o = _dot(p.astype(BF16), vb) / denom
                o_cat = jnp.concatenate([o[g_ * tq:(g_ + 1) * tq] for g_ in range(GROUP)], axis=1)
                ao_ref[pl.ds(row0, tq), kvh * GROUP * HEAD_DIM:(kvh + 1) * GROUP * HEAD_DIM] = o_cat.astype(BF16)
                yield
    if carry:
        kbuf_ref[:, :WINDOW, :] = kbuf_ref[:, tm:, :]
        vbuf_ref[:, :WINDOW, :] = vbuf_ref[:, tm:, :]


def _l1_mixer_count(ns, tm, tq):
    return 1 + ns * (tm // tq) * N_KV


def _l1_ffn_items(xres, ao_ref, w, act_ref, y_ref):
    (wo_ref, gf_ref, wg_ref, wu_ref, wd_ref) = w
    x1 = xres + _dot(ao_ref[...], wo_ref[...])
    yield
    yield from _ffn_items(x1, gf_ref, wg_ref, wu_ref, wd_ref, act_ref, y_ref)


def _l1_ffn_then_proj_items(x, x1_ref, ao_ref, w, act_ref, y_ref, *, do_ffn=True, do_proj=True):
    (wo_ref, gf_ref, wg_ref, wu_ref, wd_ref) = w
    if do_ffn:
        yield from _ffn_items(x1_ref[...], gf_ref, wg_ref, wu_ref, wd_ref, act_ref, y_ref)
    if do_proj:
        x1 = x + _dot(ao_ref[...], wo_ref[...])
        x1_ref[...] = x1
        yield


def _layer1_prompt_kernel(x_ref, sinks_ref, gm_ref, wqkv_ref, qg_ref, kg_ref, wo_ref, gf_ref,
                          wg_ref, wu_ref, wd_ref, y_ref, ko_ref, vo_ref,
                          qn_ref, kbuf_ref, vbuf_ref, ao_ref, act_ref, bias_ref, x1_ref, *, tm, n_t, n_tiles):
    i = pl.program_id(0)
    t = lax.rem(jnp.minimum(i, n_tiles - 1), n_t)

    @pl.when(i == 0)
    def _():
        _init_alibi_bias(bias_ref, CHUNK)

    @pl.when(t == 0)
    def _():
        kbuf_ref[:, :WINDOW, :] = jnp.zeros((1, WINDOW, KV_W), BF16)
        vbuf_ref[:, :WINDOW, :] = jnp.zeros((1, WINDOW, KV_W), BF16)

    def ffn(**stages):
        return _l1_ffn_then_proj_items(x_ref[0], x1_ref, ao_ref, (wo_ref, gf_ref, wg_ref, wu_ref, wd_ref),
                                       act_ref, y_ref.at[0], **stages)

    def mixer():
        return _l1_mixer_items(x_ref[0], (gm_ref, wqkv_ref, qg_ref, kg_ref), None, sinks_ref, ko_ref, vo_ref,
                               qn_ref, kbuf_ref, vbuf_ref, ao_ref, bias_ref,
                               ns=1, tm=tm, tq=CHUNK, first_pos=t * tm, carry=True)

    @pl.when(i == 0)
    def _():
        _run(mixer())
        _run(ffn(do_ffn=False))

    @pl.when((i > 0) & (i < n_tiles))
    def _():
        _interleave(ffn(), FFN_ITEMS + 1, mixer(), _l1_mixer_count(1, tm, CHUNK))

    @pl.when(i == n_tiles)
    def _():
        _run(ffn(do_proj=False))


def _layer1_sample_kernel(x_ref, kc_ref, vc_ref, sinks_ref, gm_ref, wqkv_ref, qg_ref, kg_ref, wo_ref, gf_ref,
                          wg_ref, wu_ref, wd_ref, y_ref, ko_ref, vo_ref,
                          qn_ref, kbuf_ref, vbuf_ref, ao_ref, act_ref, bias_ref, yflat_ref, *, ns, tm):
    m = ns * tm
    _init_alibi_bias(bias_ref, tm)
    kbuf_ref[:, :WINDOW, :] = kc_ref[...].astype(BF16)
    vbuf_ref[:, :WINDOW, :] = vc_ref[...].astype(BF16)
    x = x_ref[...].reshape(m, D_MODEL)
    _run(_l1_mixer_items(x, (gm_ref, wqkv_ref, qg_ref, kg_ref), (kc_ref, vc_ref), sinks_ref, ko_ref, vo_ref,
                         qn_ref, kbuf_ref, vbuf_ref, ao_ref, bias_ref,
                         ns=ns, tm=tm, tq=tm, first_pos=None, carry=False))
    _run(_l1_ffn_items(x, ao_ref, (wo_ref, gf_ref, wg_ref, wu_ref, wd_ref), act_ref, yflat_ref))
    y_ref[...] = yflat_ref[...].reshape(ns, tm, D_MODEL)


def _resident(arr, layer=None):
    if layer is None:
        nd = arr.ndim
        return pl.BlockSpec(arr.shape, lambda *_, _nd=nd: (0,) * _nd, pipeline_mode=pl.Buffered(1))
    nd = arr.ndim - 1
    return pl.BlockSpec((None,) + arr.shape[1:], lambda *_, _nd=nd, _l=layer: (_l,) + (0,) * _nd,
                        pipeline_mode=pl.Buffered(1))


def _params():
    return pltpu.CompilerParams(dimension_semantics=("arbitrary",), vmem_limit_bytes=VMEM_LIMIT)


def _tile_maps(n_t, n_tiles):
    def cur(i):
        j = jnp.minimum(i, n_tiles - 1)
        return (j // n_t, j % n_t, 0)

    def prev(i):
        j = jnp.maximum(i - 1, 0)
        return (j // n_t, j % n_t, 0)

    def cur_stream(i):
        return (jnp.minimum(i, n_tiles - 1) // n_t, 0, 0)

    return cur, prev, cur_stream


def _whole(shape):
    return pl.BlockSpec(shape, lambda i, _n=len(shape): (0,) * _n)


def _layer0_scratch(ns, tm):
    m = ns * tm
    return [pltpu.VMEM((ns, HALO + tm + BF16_ROWS, D_MODEL), F32),
            pltpu.VMEM((m, D_MODEL), F32),
            pltpu.VMEM((m, D_MODEL), BF16),
            pltpu.VMEM((m, D_FF), BF16)]


def _conv_tables(tm):
    rb = min(tm, CONV_ROWS)
    return [pltpu.VMEM((CONV_W, BF16_ROWS, D_MODEL), BF16),
            pltpu.VMEM((rb, SUBLANES * (rb + BF16_ROWS)), BF16)]


def _layer0(x, state, xs, states, w0, *, tm):
    nb, seq, _ = x.shape
    ns, tms, _ = xs.shape
    n_t = seq // tm
    n_tiles = nb * n_t
    tile = (1, tm, D_MODEL)
    last = n_tiles - 1
    tile_map = lambda i: (jnp.minimum(i, last) // n_t, jnp.minimum(i, last) % n_t, 0)
    stream_map = lambda i: (jnp.minimum(i, last) // n_t, 0, 0)
    once = lambda shape: pl.BlockSpec(shape, lambda i, _n=len(shape): (0,) * _n, pipeline_mode=pl.Buffered(1))
    return pl.pallas_call(
        functools.partial(_layer0_kernel, tm=tm, n_t=n_t, n_tiles=n_tiles),
        grid=(n_tiles + 1,),
        in_specs=[pl.BlockSpec(tile, tile_map), pl.BlockSpec((1, HALO, D_MODEL), stream_map),
                  once((ns, tms, D_MODEL)), once((ns, HALO, D_MODEL))] + [_resident(a, l) for a, l in w0],
        out_specs=[pl.BlockSpec(tile, tile_map), pl.BlockSpec((1, HALO, D_MODEL), stream_map),
                   _whole((ns, tms, D_MODEL)), _whole((ns, HALO, D_MODEL))],
        out_shape=[jax.ShapeDtypeStruct((nb, seq, D_MODEL), F32),
                   jax.ShapeDtypeStruct((nb, HALO, D_MODEL), F32),
                   jax.ShapeDtypeStruct((ns, tms, D_MODEL), F32),
                   jax.ShapeDtypeStruct((ns, HALO, D_MODEL), F32)],
        scratch_shapes=_layer0_scratch(1, tm) + _conv_tables(tm)
                       + [pltpu.VMEM((ns, HALO + tms + BF16_ROWS, D_MODEL), F32), _conv_tables(tms)[1],
                          pltpu.VMEM((ns * tms, D_MODEL), F32)],
        compiler_params=_params(),
        name="layer0",
    )(x, state, xs, states, *[a for a, _ in w0])


def _layer1_scratch(ns, tm, tq):
    m = ns * tm
    return [pltpu.VMEM((m, D_MODEL), BF16),
            pltpu.VMEM((ns, WINDOW + tm, KV_W), BF16),
            pltpu.VMEM((ns, WINDOW + tm, KV_W), BF16),
            pltpu.VMEM((m, D_MODEL), BF16),
            pltpu.VMEM((m, D_FF), BF16),
            pltpu.VMEM((N_KV, GROUP * tq, WINDOW + tq), F32),
            pltpu.VMEM((m, D_MODEL), F32)]


def _layer1_prompt(x, sinks, w1, *, tm):
    nb, seq, _ = x.shape
    n_t = seq // tm
    n_tiles = nb * n_t
    cur, prev, cur_stream = _tile_maps(n_t, n_tiles)
    tile = (1, tm, D_MODEL)
    cache = pl.BlockSpec((1, WINDOW, KV_W), cur_stream)
    return pl.pallas_call(
        functools.partial(_layer1_prompt_kernel, tm=tm, n_t=n_t, n_tiles=n_tiles),
        grid=(n_tiles + 1,),
        in_specs=[pl.BlockSpec(tile, cur), pl.BlockSpec(memory_space=pltpu.SMEM)]
                 + [_resident(a, l) for a, l in w1],
        out_specs=[pl.BlockSpec(tile, prev), cache, cache],
        out_shape=[jax.ShapeDtypeStruct((nb, seq, D_MODEL), F32),
                   jax.ShapeDtypeStruct((nb, WINDOW, KV_W), F32),
                   jax.ShapeDtypeStruct((nb, WINDOW, KV_W), F32)],
        scratch_shapes=_layer1_scratch(1, tm, CHUNK),
        compiler_params=_params(),
        name="layer1_prompt",
    )(x, sinks, *[a for a, _ in w1])


def _layer1_sample(x, kc, vc, sinks, w1):
    ns, tm, _ = x.shape
    cache = _whole((ns, WINDOW, KV_W))
    return pl.pallas_call(
        functools.partial(_layer1_sample_kernel, ns=ns, tm=tm),
        grid=(1,),
        in_specs=[_whole((ns, tm, D_MODEL)), cache, cache, pl.BlockSpec(memory_space=pltpu.SMEM)]
                 + [_resident(a, l) for a, l in w1],
        out_specs=[_whole((ns, tm, D_MODEL)), cache, cache],
        out_shape=[jax.ShapeDtypeStruct((ns, tm, D_MODEL), F32),
                   jax.ShapeDtypeStruct((ns, WINDOW, KV_W), F32),
                   jax.ShapeDtypeStruct((ns, WINDOW, KV_W), F32)],
        scratch_shapes=_layer1_scratch(ns, tm, tm),
        compiler_params=_params(),
        name="layer1_sample",
    )(x, kc, vc, sinks, *[a for a, _ in w1])


def kernel(x_prompt, x_sample, state_conv, cache_k, cache_v, g_mix, g_ffn, w_pw1, b_pw1, w_dw, b_dw,
           ln_g, ln_b, w_pw2, b_pw2, w_qkv, qn_g, kn_g, sinks, w_o, w_gate, w_up, w_down):
    nb = x_prompt.shape[0]
    ndb = x_sample.shape[0]
    row = lambda a: (a.reshape(1, -1), None)
    wg, wu, wd = w_gate.astype(BF16), w_up.astype(BF16), w_down.astype(BF16)
    w0 = [row(g_mix[0]), (w_pw1.astype(BF16), 0), row(b_pw1[0]), (w_dw, 0), row(b_dw[0]), row(ln_g[0]),
          row(ln_b[0]), (w_pw2.astype(BF16), 0), row(b_pw2[0]), row(g_ffn[0]), (wg, 0), (wu, 0), (wd, 0)]
    w1 = [row(g_mix[1]), (w_qkv.astype(BF16), 0), row(jnp.tile(qn_g[0], N_HEADS)), row(jnp.tile(kn_g[0], N_KV)),
          (w_o.astype(BF16), 0), row(g_ffn[1]), (wg, 1), (wu, 1), (wd, 1)]

    zero_state = jnp.zeros((nb, HALO, D_MODEL), F32)
    sample_state = jnp.pad(state_conv[0], ((0, 0), (HALO_PAD, 0), (0, 0)))
    xp1, cst_p, xs1, cst_s = _layer0(x_prompt, zero_state, x_sample, sample_state, w0, tm=PROMPT_TM)

    kc = cache_k[0].reshape(ndb, WINDOW, KV_W)
    vc = cache_v[0].reshape(ndb, WINDOW, KV_W)
    yp, kp, vp = _layer1_prompt(xp1, sinks[0], w1, tm=PROMPT_TM)
    ys, ks, vs = _layer1_sample(xs1, kc, vc, sinks[0], w1)

    kv_shape = lambda a: a.reshape(1, a.shape[0], WINDOW, N_KV, HEAD_DIM)
    return (yp, ys, cst_p[None, :, HALO_PAD:, :], cst_s[None, :, HALO_PAD:, :],
            kv_shape(kp), kv_shape(vp), kv_shape(ks), kv_shape(vs))
```

```python
import functools

import jax
import jax.numpy as jnp
from jax import lax
from jax.experimental import pallas as pl
from jax.experimental.pallas import tpu as pltpu

D_MODEL = 1024
CHUNK = 64
CONV_W = 31
HEAD_DIM = 64
N_HEADS = D_MODEL // HEAD_DIM
N_KV = 4
GROUP = N_HEADS // N_KV
WINDOW = 128
D_FF = 2816
KV_W = N_KV * HEAD_DIM
SCALE = HEAD_DIM ** -0.5
NEG = -1e30
RMS_EPS = 1e-6
LN_EPS = 1e-5

SUBLANES = 8
HALO = -(-(CONV_W - 1) // SUBLANES) * SUBLANES
HALO_PAD = HALO - (CONV_W - 1)
MXU_N = 256
CONV_ROWS = 128
CONV_COLS = 256
BF16_ROWS = 16
LN_ROWS = 64
PW1_ROWS = 256
VMEM_LIMIT = 56 * 1024 * 1024
PROMPT_TM = 512
PROMPT_TM_L1 = 256

F32 = jnp.float32
BF16 = jnp.bfloat16


def _dot(a, b):
    return jnp.dot(a, b, preferred_element_type=F32)


def _rms(x, g):
    ms = jnp.mean(x * x, axis=-1, keepdims=True)
    return x * lax.rsqrt(ms + RMS_EPS) * g


def _run(gen):
    for _ in gen:
        pass


def _interleave(main, n_main, side, n_side):
    next(side)
    done = 1
    for i in range(n_main):
        next(main)
        want = max(1, min(n_side, -(-(n_side * (i + 1)) // (n_main - 1))))
        for _ in range(want - done):
            next(side)
        done = want
    _run(main)
    _run(side)


FFN_ITEMS = D_FF // MXU_N + D_MODEL // MXU_N


def _ffn_items(x1, gf_ref, wg_ref, wu_ref, wd_ref, act_ref, y_ref):
    hn = _rms(x1, gf_ref[...]).astype(BF16)
    for j in range(D_FF // MXU_N):
        sl = slice(j * MXU_N, (j + 1) * MXU_N)
        gt = _dot(hn, wg_ref[:, sl])
        up = _dot(hn, wu_ref[:, sl])
        act_ref[:, sl] = (gt * jax.nn.sigmoid(gt) * up).astype(BF16)
        yield
    for j in range(D_MODEL // MXU_N):
        sl = slice(j * MXU_N, (j + 1) * MXU_N)
        dn = _dot(act_ref[...], wd_ref[:, sl])
        y_ref[:, sl] = x1[:, sl] + dn
        yield


def _init_conv_tables(wdw_ref, wb_ref, shift_ref):
    for k in range(CONV_W):
        wb_ref[k] = jnp.broadcast_to(wdw_ref[k:k + 1, :], (BF16_ROWS, D_MODEL)).astype(BF16)
    rb, cols = shift_ref.shape
    pr = cols // SUBLANES
    r = lax.broadcasted_iota(jnp.int32, (rb, cols), 0)
    c = lax.broadcasted_iota(jnp.int32, (rb, cols), 1)
    shift_ref[...] = jnp.where(c % pr == r + c // pr, 1.0, 0.0).astype(BF16)


def _l0_mixer(x, w, gext_ref, cbuf_ref, sw_ref, wb_ref, shift_ref, *, ns, tm, carry):
    (gm_ref, wpw1_ref, bpw1_ref, bdw_ref, lng_ref, lnb_ref) = w
    m = ns * tm
    pb = min(tm, PW1_ROWS) if ns == 1 else m
    for q0 in range(0, m, pb):
        h = _rms(x[q0:q0 + pb], gm_ref[...]).astype(BF16)
        a = _dot(h, wpw1_ref[:, :D_MODEL]) + bpw1_ref[:, :D_MODEL]
        gate = _dot(h, wpw1_ref[:, D_MODEL:]) + bpw1_ref[:, D_MODEL:]
        g = a * jax.nn.sigmoid(gate)
        if ns == 1:
            gext_ref[0, HALO + q0:HALO + q0 + pb, :] = g
        else:
            gext_ref[:, HALO:HALO + tm, :] = g.reshape(ns, tm, D_MODEL)

    rb = min(tm, CONV_ROWS)
    pr = rb + BF16_ROWS
    nz = rb + HALO + BF16_ROWS
    for s in range(ns):
        for r0 in range(0, tm, rb):
            for c0 in range(0, D_MODEL, CONV_COLS):
                cs = slice(c0, c0 + CONV_COLS)
                z = gext_ref[s, pl.ds(r0, nz), cs]
                zb = (z.astype(BF16), z[SUBLANES:nz - SUBLANES].astype(BF16))
                parts = []
                for p in range(SUBLANES):
                    part = None
                    for k in range(CONV_W):
                        if (HALO_PAD + k) % SUBLANES != p:
                            continue
                        odd = ((HALO_PAD + k - p) // SUBLANES) % 2
                        off = HALO_PAD + k - p - SUBLANES * odd
                        wt = jnp.concatenate([wb_ref[k, :, cs]] * (pr // BF16_ROWS), axis=0)
                        term = zb[odd][off:off + pr] * wt
                        part = term if part is None else part + term
                    parts.append(part)
                acc = _dot(shift_ref[...], jnp.concatenate(parts, axis=0))
                cbuf_ref[pl.ds(s * tm + r0, rb), cs] = acc + bdw_ref[:, cs]
    if carry:
        gext_ref[:, :HALO, :] = gext_ref[:, tm:tm + HALO, :]

    lr = min(m, LN_ROWS)
    for r0 in range(0, m, lr):
        c = cbuf_ref[pl.ds(r0, lr), :]
        mu = jnp.mean(c, axis=-1, keepdims=True)
        xc = c - mu
        ln = xc * lax.rsqrt(jnp.mean(xc * xc, axis=-1, keepdims=True) + LN_EPS) * lng_ref[...] + lnb_ref[...]
        sw_ref[pl.ds(r0, lr), :] = (ln * jax.nn.sigmoid(ln)).astype(BF16)


def _l0_ffn(xres, sw_ref, w, act_ref, y_ref):
    (wpw2_ref, bpw2_ref, gf_ref, wg_ref, wu_ref, wd_ref) = w
    x1 = xres + _dot(sw_ref[...], wpw2_ref[...]) + bpw2_ref[...]
    _run(_ffn_items(x1, gf_ref, wg_ref, wu_ref, wd_ref, act_ref, y_ref))


def _layer0_kernel(x_ref, st_ref, xs_ref, sts_ref, gm_ref, wpw1_ref, bpw1_ref, wdw_ref, bdw_ref,
                   lng_ref, lnb_ref, wpw2_ref, bpw2_ref, gf_ref, wg_ref, wu_ref, wd_ref,
                   y_ref, nst_ref, ys_ref, nsts_ref,
                   gext_ref, cbuf_ref, sw_ref, act_ref, wb_ref, shift_ref, gexts_ref, shifts_ref, yflat_ref,
                   *, tm, n_t, n_tiles):
    i = pl.program_id(0)
    t = lax.rem(i, n_t)
    ns, tms, _ = xs_ref.shape
    ms = ns * tms
    mix_w = (gm_ref, wpw1_ref, bpw1_ref, bdw_ref, lng_ref, lnb_ref)
    ffn_w = (wpw2_ref, bpw2_ref, gf_ref, wg_ref, wu_ref, wd_ref)

    @pl.when(i == 0)
    def _():
        _init_conv_tables(wdw_ref, wb_ref, shift_ref)
        _init_conv_tables(wdw_ref, wb_ref, shifts_ref)
        gext_ref[:, HALO + tm:, :] = jnp.zeros((1, BF16_ROWS, D_MODEL), F32)
        gexts_ref[:, HALO + tms:, :] = jnp.zeros((ns, BF16_ROWS, D_MODEL), F32)

    @pl.when((t == 0) & (i < n_tiles))
    def _():
        gext_ref[:, :HALO, :] = st_ref[...]

    @pl.when(i < n_tiles)
    def _():
        x = x_ref[0]
        _l0_mixer(x, mix_w, gext_ref, cbuf_ref, sw_ref, wb_ref, shift_ref, ns=1, tm=tm, carry=True)
        _l0_ffn(x, sw_ref, ffn_w, act_ref, y_ref.at[0])

    @pl.when((t == n_t - 1) & (i < n_tiles))
    def _():
        nst_ref[...] = gext_ref[:, tm:tm + HALO, :]

    @pl.when(i == n_tiles)
    def _():
        gexts_ref[:, :HALO, :] = sts_ref[...]
        x = xs_ref[...].reshape(ms, D_MODEL)
        _l0_mixer(x, mix_w, gexts_ref, cbuf_ref.at[:ms], sw_ref.at[:ms], wb_ref, shifts_ref, ns=ns, tm=tms,
                  carry=False)
        nsts_ref[...] = gexts_ref[:, tms:tms + HALO, :]
        _l0_ffn(x, sw_ref.at[:ms], ffn_w, act_ref.at[:ms], yflat_ref)
        ys_ref[...] = yflat_ref[...].reshape(ns, tms, D_MODEL)


def _head_mean_square(z):
    r = lax.broadcasted_iota(jnp.int32, (MXU_N, MXU_N), 0) // HEAD_DIM
    c = lax.broadcasted_iota(jnp.int32, (MXU_N, MXU_N), 1) // HEAD_DIM
    ones_bd = jnp.where(r == c, 1.0, 0.0).astype(BF16)
    outs = []
    for j in range(z.shape[1] // MXU_N):
        zz = z[:, j * MXU_N:(j + 1) * MXU_N]
        outs.append(_dot((zz * zz).astype(BF16), ones_bd))
    ss = outs[0] if len(outs) == 1 else jnp.concatenate(outs, axis=1)
    return ss * (1.0 / HEAD_DIM)


def _init_alibi_bias(bias_ref, tq):
    rows4, nk = bias_ref.shape[1:]
    rows = lax.broadcasted_iota(jnp.int32, (rows4, nk), 0)
    cols = lax.broadcasted_iota(jnp.int32, (rows4, nk), 1)
    dist = jnp.abs((rows % tq) + WINDOW - cols).astype(F32)
    grp = rows // tq
    for kvh in range(N_KV):
        slope = jnp.zeros((rows4, nk), F32)
        for g_ in range(GROUP):
            slope = jnp.where(grp == g_, 2.0 ** (-8.0 * (kvh * GROUP + g_ + 1) / N_HEADS), slope)
        bias_ref[kvh] = -(slope * dist)


def _l1_mixer_items(x, w, caches, sinks_ref, ko_ref, vo_ref, qn_ref, kbuf_ref, vbuf_ref, ao_ref, bias_ref,
                    *, ns, tm, tq, first_pos, carry):
    (gm_ref, wqkv_ref, qg_ref, kg_ref) = w
    nk = WINDOW + tq
    rows4 = GROUP * tq
    h = _rms(x, gm_ref[...]).astype(BF16)
    q = _dot(h, wqkv_ref[:, :D_MODEL])
    k = _dot(h, wqkv_ref[:, D_MODEL:D_MODEL + KV_W])
    v = _dot(h, wqkv_ref[:, D_MODEL + KV_W:])
    qn = q * lax.rsqrt(_head_mean_square(q) + RMS_EPS) * (qg_ref[...] * SCALE)
    kn = k * lax.rsqrt(_head_mean_square(k) + RMS_EPS) * kg_ref[...]
    qn_ref[...] = qn.astype(BF16)
    kbuf_ref[:, WINDOW:, :] = kn.astype(BF16).reshape(ns, tm, KV_W)
    vbuf_ref[:, WINDOW:, :] = v.astype(BF16).reshape(ns, tm, KV_W)
    kn3 = kn.reshape(ns, tm, KV_W)
    v3 = v.reshape(ns, tm, KV_W)
    if tm >= WINDOW:
        ko_ref[...] = kn3[:, tm - WINDOW:, :]
        vo_ref[...] = v3[:, tm - WINDOW:, :]
    else:
        kc_ref, vc_ref = caches
        ko_ref[:, :WINDOW - tm, :] = kc_ref[:, tm:, :]
        vo_ref[:, :WINDOW - tm, :] = vc_ref[:, tm:, :]
        ko_ref[:, WINDOW - tm:, :] = kn3
        vo_ref[:, WINDOW - tm:, :] = v3
    yield

    grp_col = lax.broadcasted_iota(jnp.int32, (rows4, 1), 0) // tq
    sink_cols = []
    for kvh in range(N_KV):
        sc = jnp.zeros((rows4, 1), F32)
        for g_ in range(GROUP):
            sc = jnp.where(grp_col == g_, sinks_ref[kvh * GROUP + g_], sc)
        sink_cols.append(sc)
    key_col = lax.broadcasted_iota(jnp.int32, (rows4, nk), 1)

    for s in range(ns):
        for c in range(tm // tq):
            r0 = c * tq
            row0 = s * tm + r0
            for kvh in range(N_KV):
                q4 = jnp.concatenate(
                    [qn_ref[pl.ds(row0, tq), (kvh * GROUP + g_) * HEAD_DIM:(kvh * GROUP + g_ + 1) * HEAD_DIM]
                     for g_ in range(GROUP)], axis=0)
                ks = slice(kvh * HEAD_DIM, (kvh + 1) * HEAD_DIM)
                kb = kbuf_ref[s, pl.ds(r0, nk), ks]
                vb = vbuf_ref[s, pl.ds(r0, nk), ks]
                sc = lax.dot_general(q4, kb, (((1,), (1,)), ((), ())), preferred_element_type=F32)
                sc = sc + bias_ref[kvh]
                if first_pos is not None:
                    sc = jnp.where(key_col >= WINDOW - r0 - first_pos, sc, NEG)
                sink = sink_cols[kvh]
                mx = jnp.maximum(jnp.max(sc, axis=-1, keepdims=True), sink)
                p = jnp.exp(sc - mx)
                denom = jnp.sum(p, axis=-1, keepdims=True) + jnp.exp(sink - mx)
                o = _dot(p.astype(BF16), vb) / denom
                o_cat = jnp.concatenate([o[g_ * tq:(g_ + 1) * tq] for g_ in range(GROUP)], axis=1)
                ao_ref[pl.ds(row0, tq), kvh * GROUP * HEAD_DIM:(kvh + 1) * GROUP * HEAD_DIM] = o_cat.astype(BF16)
                yield
    if carry:
        kbuf_ref[:, :WINDOW, :] = kbuf_ref[:, tm:, :]
        vbuf_ref[:, :WINDOW, :] = vbuf_ref[:, tm:, :]


def _l1_mixer_count(ns, tm, tq):
    return 1 + ns * (tm // tq) * N_KV


def _l1_ffn_items(xres, ao_ref, w, act_ref, y_ref):
    (wo_ref, gf_ref, wg_ref, wu_ref, wd_ref) = w
    x1 = xres + _dot(ao_ref[...], wo_ref[...])
    yield
    yield from _ffn_items(x1, gf_ref, wg_ref, wu_ref, wd_ref, act_ref, y_ref)


def _l1_ffn_then_proj_items(x, x1_ref, ao_ref, w, act_ref, y_ref, *, do_ffn=True, do_proj=True):
    (wo_ref, gf_ref, wg_ref, wu_ref, wd_ref) = w
    if do_ffn:
        yield from _ffn_items(x1_ref[...], gf_ref, wg_ref, wu_ref, wd_ref, act_ref, y_ref)
    if do_proj:
        x1 = x + _dot(ao_ref[...], wo_ref[...])
        x1_ref[...] = x1
        yield


def _layer1_kernel(x_ref, xs_ref, kc_ref, vc_ref, sinks_ref, gm_ref, wqkv_ref, qg_ref, kg_ref, wo_ref, gf_ref,
                   wg_ref, wu_ref, wd_ref, y_ref, ko_ref, vo_ref, ys_ref, kos_ref, vos_ref,
                   qn_ref, kbuf_ref, vbuf_ref, ao_ref, act_ref, bias_ref, x1_ref,
                   kbufs_ref, vbufs_ref, biass_ref, *, tm, n_t, n_tiles):
    i = pl.program_id(0)
    t = lax.rem(jnp.minimum(i, n_tiles - 1), n_t)

    @pl.when(i == 0)
    def _():
        _init_alibi_bias(bias_ref, CHUNK)

    @pl.when(i == n_tiles + 1)
    def _():
        ns, tms, _ = xs_ref.shape
        ms = ns * tms
        _init_alibi_bias(biass_ref, tms)
        kbufs_ref[:, :WINDOW, :] = kc_ref[...].astype(BF16)
        vbufs_ref[:, :WINDOW, :] = vc_ref[...].astype(BF16)
        xs = xs_ref[...].reshape(ms, D_MODEL)
        _run(_l1_mixer_items(xs, (gm_ref, wqkv_ref, qg_ref, kg_ref), (kc_ref, vc_ref), sinks_ref, kos_ref, vos_ref,
                             qn_ref.at[:ms], kbufs_ref, vbufs_ref, ao_ref.at[:ms], biass_ref,
                             ns=ns, tm=tms, tq=tms, first_pos=None, carry=False))
        yflat_ref = x1_ref.at[:ms]
        _run(_l1_ffn_items(xs, ao_ref.at[:ms], (wo_ref, gf_ref, wg_ref, wu_ref, wd_ref), act_ref.at[:ms], yflat_ref))
        ys_ref[...] = yflat_ref[...].reshape(ns, tms, D_MODEL)

    @pl.when(t == 0)
    def _():
        kbuf_ref[:, :WINDOW, :] = jnp.zeros((1, WINDOW, KV_W), BF16)
        vbuf_ref[:, :WINDOW, :] = jnp.zeros((1, WINDOW, KV_W), BF16)

    def ffn(**stages):
        return _l1_ffn_then_proj_items(x_ref[0], x1_ref, ao_ref, (wo_ref, gf_ref, wg_ref, wu_ref, wd_ref),
                                       act_ref, y_ref.at[0], **stages)

    def mixer():
        return _l1_mixer_items(x_ref[0], (gm_ref, wqkv_ref, qg_ref, kg_ref), None, sinks_ref, ko_ref, vo_ref,
                               qn_ref, kbuf_ref, vbuf_ref, ao_ref, bias_ref,
                               ns=1, tm=tm, tq=CHUNK, first_pos=t * tm, carry=True)

    @pl.when(i == 0)
    def _():
        _run(mixer())
        _run(ffn(do_ffn=False))

    @pl.when((i > 0) & (i < n_tiles))
    def _():
        _interleave(ffn(), FFN_ITEMS + 1, mixer(), _l1_mixer_count(1, tm, CHUNK))

    @pl.when(i == n_tiles)
    def _():
        _run(ffn(do_proj=False))


def _resident(arr, layer=None):
    if layer is None:
        nd = arr.ndim
        return pl.BlockSpec(arr.shape, lambda *_, _nd=nd: (0,) * _nd, pipeline_mode=pl.Buffered(1))
    nd = arr.ndim - 1
    return pl.BlockSpec((None,) + arr.shape[1:], lambda *_, _nd=nd, _l=layer: (_l,) + (0,) * _nd,
                        pipeline_mode=pl.Buffered(1))


def _params():
    return pltpu.CompilerParams(dimension_semantics=("arbitrary",), vmem_limit_bytes=VMEM_LIMIT)


def _tile_maps(n_t, n_tiles):
    def cur(i):
        j = jnp.minimum(i, n_tiles - 1)
        return (j // n_t, j % n_t, 0)

    def prev(i):
        j = jnp.clip(i - 1, 0, n_tiles - 1)
        return (j // n_t, j % n_t, 0)

    def cur_stream(i):
        return (jnp.minimum(i, n_tiles - 1) // n_t, 0, 0)

    return cur, prev, cur_stream


def _whole(shape):
    return pl.BlockSpec(shape, lambda i, _n=len(shape): (0,) * _n)


def _layer0_scratch(ns, tm):
    m = ns * tm
    return [pltpu.VMEM((ns, HALO + tm + BF16_ROWS, D_MODEL), F32),
            pltpu.VMEM((m, D_MODEL), F32),
            pltpu.VMEM((m, D_MODEL), BF16),
            pltpu.VMEM((m, D_FF), BF16)]


def _conv_tables(tm):
    rb = min(tm, CONV_ROWS)
    return [pltpu.VMEM((CONV_W, BF16_ROWS, D_MODEL), BF16),
            pltpu.VMEM((rb, SUBLANES * (rb + BF16_ROWS)), BF16)]


def _layer0(x, state, xs, states, w0, *, tm):
    nb, seq, _ = x.shape
    ns, tms, _ = xs.shape
    n_t = seq // tm
    n_tiles = nb * n_t
    tile = (1, tm, D_MODEL)
    last = n_tiles - 1
    tile_map = lambda i: (jnp.minimum(i, last) // n_t, jnp.minimum(i, last) % n_t, 0)
    stream_map = lambda i: (jnp.minimum(i, last) // n_t, 0, 0)
    once = lambda shape: pl.BlockSpec(shape, lambda i, _n=len(shape): (0,) * _n, pipeline_mode=pl.Buffered(1))
    return pl.pallas_call(
        functools.partial(_layer0_kernel, tm=tm, n_t=n_t, n_tiles=n_tiles),
        grid=(n_tiles + 1,),
        in_specs=[pl.BlockSpec(tile, tile_map), pl.BlockSpec((1, HALO, D_MODEL), stream_map),
                  once((ns, tms, D_MODEL)), once((ns, HALO, D_MODEL))] + [_resident(a, l) for a, l in w0],
        out_specs=[pl.BlockSpec(tile, tile_map), pl.BlockSpec((1, HALO, D_MODEL), stream_map),
                   _whole((ns, tms, D_MODEL)), _whole((ns, HALO, D_MODEL))],
        out_shape=[jax.ShapeDtypeStruct((nb, seq, D_MODEL), F32),
                   jax.ShapeDtypeStruct((nb, HALO, D_MODEL), F32),
                   jax.ShapeDtypeStruct((ns, tms, D_MODEL), F32),
                   jax.ShapeDtypeStruct((ns, HALO, D_MODEL), F32)],
        scratch_shapes=_layer0_scratch(1, tm) + _conv_tables(tm)
                       + [pltpu.VMEM((ns, HALO + tms + BF16_ROWS, D_MODEL), F32), _conv_tables(tms)[1],
                          pltpu.VMEM((ns * tms, D_MODEL), F32)],
        compiler_params=_params(),
        name="layer0",
    )(x, state, xs, states, *[a for a, _ in w0])


def _layer1_scratch(ns, tm, tq):
    m = ns * tm
    return [pltpu.VMEM((m, D_MODEL), BF16),
            pltpu.VMEM((ns, WINDOW + tm, KV_W), BF16),
            pltpu.VMEM((ns, WINDOW + tm, KV_W), BF16),
            pltpu.VMEM((m, D_MODEL), BF16),
            pltpu.VMEM((m, D_FF), BF16),
            pltpu.VMEM((N_KV, GROUP * tq, WINDOW + tq), F32),
            pltpu.VMEM((m, D_MODEL), F32)]


def _layer1(x, xs, kc, vc, sinks, w1, *, tm):
    nb, seq, _ = x.shape
    ns, tms, _ = xs.shape
    assert tm >= ns * tms, "the sample step borrows the prompt tile's scratch rows"
    n_t = seq // tm
    n_tiles = nb * n_t
    cur, prev, cur_stream = _tile_maps(n_t, n_tiles)
    tile = (1, tm, D_MODEL)
    cache = pl.BlockSpec((1, WINDOW, KV_W), cur_stream)
    once = lambda shape: pl.BlockSpec(shape, lambda i, _n=len(shape): (0,) * _n, pipeline_mode=pl.Buffered(1))
    sample_scratch = _layer1_scratch(ns, tms, tms)
    return pl.pallas_call(
        functools.partial(_layer1_kernel, tm=tm, n_t=n_t, n_tiles=n_tiles),
        grid=(n_tiles + 2,),
        in_specs=[pl.BlockSpec(tile, cur), once((ns, tms, D_MODEL)), once((ns, WINDOW, KV_W)),
                  once((ns, WINDOW, KV_W)), pl.BlockSpec(memory_space=pltpu.SMEM)]
                 + [_resident(a, l) for a, l in w1],
        out_specs=[pl.BlockSpec(tile, prev), cache, cache,
                   _whole((ns, tms, D_MODEL)), _whole((ns, WINDOW, KV_W)), _whole((ns, WINDOW, KV_W))],
        out_shape=[jax.ShapeDtypeStruct((nb, seq, D_MODEL), F32),
                   jax.ShapeDtypeStruct((nb, WINDOW, KV_W), F32),
                   jax.ShapeDtypeStruct((nb, WINDOW, KV_W), F32),
                   jax.ShapeDtypeStruct((ns, tms, D_MODEL), F32),
                   jax.ShapeDtypeStruct((ns, WINDOW, KV_W), F32),
                   jax.ShapeDtypeStruct((ns, WINDOW, KV_W), F32)],
        scratch_shapes=_layer1_scratch(1, tm, CHUNK) + [sample_scratch[1], sample_scratch[2], sample_scratch[5]],
        compiler_params=_params(),
        name="layer1",
    )(x, xs, kc, vc, sinks, *[a for a, _ in w1])


def kernel(x_prompt, x_sample, state_conv, cache_k, cache_v, g_mix, g_ffn, w_pw1, b_pw1, w_dw, b_dw,
           ln_g, ln_b, w_pw2, b_pw2, w_qkv, qn_g, kn_g, sinks, w_o, w_gate, w_up, w_down):
    nb = x_prompt.shape[0]
    ndb = x_sample.shape[0]
    row = lambda a: (a.reshape(1, -1), None)
    wg, wu, wd = w_gate.astype(BF16), w_up.astype(BF16), w_down.astype(BF16)
    w0 = [row(g_mix[0]), (w_pw1.astype(BF16), 0), row(b_pw1[0]), (w_dw, 0), row(b_dw[0]), row(ln_g[0]),
          row(ln_b[0]), (w_pw2.astype(BF16), 0), row(b_pw2[0]), row(g_ffn[0]), (wg, 0), (wu, 0), (wd, 0)]
    w1 = [row(g_mix[1]), (w_qkv.astype(BF16), 0), row(jnp.tile(qn_g[0], N_HEADS)), row(jnp.tile(kn_g[0], N_KV)),
          (w_o.astype(BF16), 0), row(g_ffn[1]), (wg, 1), (wu, 1), (wd, 1)]

    zero_state = jnp.zeros((nb, HALO, D_MODEL), F32)
    sample_state = jnp.pad(state_conv[0], ((0, 0), (HALO_PAD, 0), (0, 0)))
    xp1, cst_p, xs1, cst_s = _layer0(x_prompt, zero_state, x_sample, sample_state, w0, tm=PROMPT_TM)

    kc = cache_k[0].reshape(ndb, WINDOW, KV_W)
    vc = cache_v[0].reshape(ndb, WINDOW, KV_W)
    yp, kp, vp, ys, ks, vs = _layer1(xp1, xs1, kc, vc, sinks[0], w1, tm=PROMPT_TM_L1)

    kv_shape = lambda a: a.reshape(1, a.shape[0], WINDOW, N_KV, HEAD_DIM)
    return (yp, ys, cst_p[None, :, HALO_PAD:, :], cst_s[None, :, HALO_PAD:, :],
            kv_shape(kp), kv_shape(vp), kv_shape(ks), kv_shape(vs))
```

```python
import functools

import jax
import jax.numpy as jnp
from jax import lax
from jax.experimental import pallas as pl
from jax.experimental.pallas import tpu as pltpu

D_MODEL = 1024
CHUNK = 64
CONV_W = 31
HEAD_DIM = 64
N_HEADS = D_MODEL // HEAD_DIM
N_KV = 4
GROUP = N_HEADS // N_KV
WINDOW = 128
D_FF = 2816
KV_W = N_KV * HEAD_DIM
SCALE = HEAD_DIM ** -0.5
NEG = -1e30
RMS_EPS = 1e-6
LN_EPS = 1e-5

SUBLANES = 8
HALO = -(-(CONV_W - 1) // SUBLANES) * SUBLANES
HALO_PAD = HALO - (CONV_W - 1)
MXU_N = 256
CONV_ROWS = 128
CONV_COLS = 256
BF16_ROWS = 16
LN_ROWS = 64
PW1_ROWS = 256
VMEM_LIMIT = 56 * 1024 * 1024
PROMPT_TM = 512

F32 = jnp.float32
BF16 = jnp.bfloat16


def _dot(a, b):
    return jnp.dot(a, b, preferred_element_type=F32)


def _rms(x, g):
    ms = jnp.mean(x * x, axis=-1, keepdims=True)
    return x * lax.rsqrt(ms + RMS_EPS) * g


def _run(gen):
    for _ in gen:
        pass


def _interleave(main, n_main, side, n_side):
    next(side)
    done = 1
    for i in range(n_main):
        next(main)
        want = max(1, min(n_side, -(-(n_side * (i + 1)) // (n_main - 1))))
        for _ in range(want - done):
            next(side)
        done = want
    _run(main)
    _run(side)


FFN_ITEMS = D_FF // MXU_N + D_MODEL // MXU_N


def _ffn_items(x1, gf_ref, wg_ref, wu_ref, wd_ref, act_ref, y_ref):
    hn = _rms(x1, gf_ref[...]).astype(BF16)
    for j in range(D_FF // MXU_N):
        sl = slice(j * MXU_N, (j + 1) * MXU_N)
        gt = _dot(hn, wg_ref[:, sl])
        up = _dot(hn, wu_ref[:, sl])
        act_ref[:, sl] = (gt * jax.nn.sigmoid(gt) * up).astype(BF16)
        yield
    for j in range(D_MODEL // MXU_N):
        sl = slice(j * MXU_N, (j + 1) * MXU_N)
        dn = _dot(act_ref[...], wd_ref[:, sl])
        y_ref[:, sl] = x1[:, sl] + dn
        yield


def _init_conv_tables(wdw_ref, wb_ref, shift_ref):
    for k in range(CONV_W):
        wb_ref[k] = jnp.broadcast_to(wdw_ref[k:k + 1, :], (BF16_ROWS, D_MODEL)).astype(BF16)
    rb, cols = shift_ref.shape
    pr = cols // SUBLANES
    r = lax.broadcasted_iota(jnp.int32, (rb, cols), 0)
    c = lax.broadcasted_iota(jnp.int32, (rb, cols), 1)
    shift_ref[...] = jnp.where(c % pr == r + c // pr, 1.0, 0.0).astype(BF16)


def _l0_mixer(x, w, gext_ref, cbuf_ref, sw_ref, wb_ref, shift_ref, *, ns, tm, carry):
    (gm_ref, wpw1_ref, bpw1_ref, bdw_ref, lng_ref, lnb_ref) = w
    m = ns * tm
    pb = min(tm, PW1_ROWS) if ns == 1 else m
    for q0 in range(0, m, pb):
        h = _rms(x[q0:q0 + pb], gm_ref[...]).astype(BF16)
        a = _dot(h, wpw1_ref[:, :D_MODEL]) + bpw1_ref[:, :D_MODEL]
        gate = _dot(h, wpw1_ref[:, D_MODEL:]) + bpw1_ref[:, D_MODEL:]
        g = a * jax.nn.sigmoid(gate)
        if ns == 1:
            gext_ref[0, HALO + q0:HALO + q0 + pb, :] = g
        else:
            gext_ref[:, HALO:HALO + tm, :] = g.reshape(ns, tm, D_MODEL)

    rb = min(tm, CONV_ROWS)
    pr = rb + BF16_ROWS
    nz = rb + HALO + BF16_ROWS
    for s in range(ns):
        for r0 in range(0, tm, rb):
            for c0 in range(0, D_MODEL, CONV_COLS):
                cs = slice(c0, c0 + CONV_COLS)
                z = gext_ref[s, pl.ds(r0, nz), cs]
                zb = (z.astype(BF16), z[SUBLANES:nz - SUBLANES].astype(BF16))
                parts = []
                for p in range(SUBLANES):
                    part = None
                    for k in range(CONV_W):
                        if (HALO_PAD + k) % SUBLANES != p:
                            continue
                        odd = ((HALO_PAD + k - p) // SUBLANES) % 2
                        off = HALO_PAD + k - p - SUBLANES * odd
                        wt = jnp.concatenate([wb_ref[k, :, cs]] * (pr // BF16_ROWS), axis=0)
                        term = zb[odd][off:off + pr] * wt
                        part = term if part is None else part + term
                    parts.append(part)
                acc = _dot(shift_ref[...], jnp.concatenate(parts, axis=0))
                cbuf_ref[pl.ds(s * tm + r0, rb), cs] = acc + bdw_ref[:, cs]
    if carry:
        gext_ref[:, :HALO, :] = gext_ref[:, tm:tm + HALO, :]

    lr = min(m, LN_ROWS)
    for r0 in range(0, m, lr):
        c = cbuf_ref[pl.ds(r0, lr), :]
        mu = jnp.mean(c, axis=-1, keepdims=True)
        xc = c - mu
        ln = xc * lax.rsqrt(jnp.mean(xc * xc, axis=-1, keepdims=True) + LN_EPS) * lng_ref[...] + lnb_ref[...]
        sw_ref[pl.ds(r0, lr), :] = (ln * jax.nn.sigmoid(ln)).astype(BF16)


def _l0_ffn(xres, sw_ref, w, act_ref, y_ref):
    (wpw2_ref, bpw2_ref, gf_ref, wg_ref, wu_ref, wd_ref) = w
    x1 = xres + _dot(sw_ref[...], wpw2_ref[...]) + bpw2_ref[...]
    _run(_ffn_items(x1, gf_ref, wg_ref, wu_ref, wd_ref, act_ref, y_ref))


def _layer0_kernel(x_ref, st_ref, xs_ref, sts_ref, gm_ref, wpw1_ref, bpw1_ref, wdw_ref, bdw_ref,
                   lng_ref, lnb_ref, wpw2_ref, bpw2_ref, gf_ref, wg_ref, wu_ref, wd_ref,
                   y_ref, nst_ref, ys_ref, nsts_ref,
                   gext_ref, cbuf_ref, sw_ref, act_ref, wb_ref, shift_ref, gexts_ref, yflat_ref, shifts_ref,
                   *, tm, n_t, n_tiles):
    i = pl.program_id(0)
    t = lax.rem(i, n_t)
    ns, tms, _ = xs_ref.shape
    ms = ns * tms
    mix_w = (gm_ref, wpw1_ref, bpw1_ref, bdw_ref, lng_ref, lnb_ref)
    ffn_w = (wpw2_ref, bpw2_ref, gf_ref, wg_ref, wu_ref, wd_ref)

    @pl.when(i == 0)
    def _():
        _init_conv_tables(wdw_ref, wb_ref, shift_ref)
        _init_conv_tables(wdw_ref, wb_ref, shifts_ref)
        gext_ref[:, HALO + tm:, :] = jnp.zeros((1, BF16_ROWS, D_MODEL), F32)
        gexts_ref[:, HALO + tms:, :] = jnp.zeros((ns, BF16_ROWS, D_MODEL), F32)

    @pl.when((t == 0) & (i < n_tiles))
    def _():
        gext_ref[:, :HALO, :] = st_ref[...]

    @pl.when(i < n_tiles)
    def _():
        x = x_ref[0]
        _l0_mixer(x, mix_w, gext_ref, cbuf_ref, sw_ref, wb_ref, shift_ref, ns=1, tm=tm, carry=True)
        _l0_ffn(x, sw_ref, ffn_w, act_ref, y_ref.at[0])

    @pl.when((t == n_t - 1) & (i < n_tiles))
    def _():
        nst_ref[...] = gext_ref[:, tm:tm + HALO, :]

    @pl.when(i == n_tiles)
    def _():
        gexts_ref[:, :HALO, :] = sts_ref[...]
        x = xs_ref[...].reshape(ms, D_MODEL)
        _l0_mixer(x, mix_w, gexts_ref, cbuf_ref.at[:ms], sw_ref.at[:ms], wb_ref, shifts_ref, ns=ns, tm=tms,
                  carry=False)
        nsts_ref[...] = gexts_ref[:, tms:tms + HALO, :]
        _l0_ffn(x, sw_ref.at[:ms], ffn_w, act_ref.at[:ms], yflat_ref)
        ys_ref[...] = yflat_ref[...].reshape(ns, tms, D_MODEL)


def _head_mean_square(z):
    r = lax.broadcasted_iota(jnp.int32, (MXU_N, MXU_N), 0) // HEAD_DIM
    c = lax.broadcasted_iota(jnp.int32, (MXU_N, MXU_N), 1) // HEAD_DIM
    ones_bd = jnp.where(r == c, 1.0, 0.0).astype(BF16)
    outs = []
    for j in range(z.shape[1] // MXU_N):
        zz = z[:, j * MXU_N:(j + 1) * MXU_N]
        outs.append(_dot((zz * zz).astype(BF16), ones_bd))
    ss = outs[0] if len(outs) == 1 else jnp.concatenate(outs, axis=1)
    return ss * (1.0 / HEAD_DIM)


def _init_alibi_bias(bias_ref, tq):
    rows4, nk = bias_ref.shape[1:]
    rows = lax.broadcasted_iota(jnp.int32, (rows4, nk), 0)
    cols = lax.broadcasted_iota(jnp.int32, (rows4, nk), 1)
    dist = jnp.abs((rows % tq) + WINDOW - cols).astype(F32)
    grp = rows // tq
    for kvh in range(N_KV):
        slope = jnp.zeros((rows4, nk), F32)
        for g_ in range(GROUP):
            slope = jnp.where(grp == g_, 2.0 ** (-8.0 * (kvh * GROUP + g_ + 1) / N_HEADS), slope)
        bias_ref[kvh] = -(slope * dist)


def _l1_mixer_items(x, w, caches, sinks_ref, ko_ref, vo_ref, qn_ref, kbuf_ref, vbuf_ref, ao_ref, bias_ref,
                    *, ns, tm, tq, first_pos, carry):
    (gm_ref, wqkv_ref, qg_ref, kg_ref) = w
    nk = WINDOW + tq
    rows4 = GROUP * tq
    h = _rms(x, gm_ref[...]).astype(BF16)
    q = _dot(h, wqkv_ref[:, :D_MODEL])
    k = _dot(h, wqkv_ref[:, D_MODEL:D_MODEL + KV_W])
    v = _dot(h, wqkv_ref[:, D_MODEL + KV_W:])
    qn = q * lax.rsqrt(_head_mean_square(q) + RMS_EPS) * (qg_ref[...] * SCALE)
    kn = k * lax.rsqrt(_head_mean_square(k) + RMS_EPS) * kg_ref[...]
    qn_ref[...] = qn.astype(BF16)
    kbuf_ref[:, WINDOW:, :] = kn.astype(BF16).reshape(ns, tm, KV_W)
    vbuf_ref[:, WINDOW:, :] = v.astype(BF16).reshape(ns, tm, KV_W)
    kn3 = kn.reshape(ns, tm, KV_W)
    v3 = v.reshape(ns, tm, KV_W)
    if tm >= WINDOW:
        ko_ref[...] = kn3[:, tm - WINDOW:, :]
        vo_ref[...] = v3[:, tm - WINDOW:, :]
    else:
        kc_ref, vc_ref = caches
        ko_ref[:, :WINDOW - tm, :] = kc_ref[:, tm:, :]
        vo_ref[:, :WINDOW - tm, :] = vc_ref[:, tm:, :]
        ko_ref[:, WINDOW - tm:, :] = kn3
        vo_ref[:, WINDOW - tm:, :] = v3
    yield

    grp_col = lax.broadcasted_iota(jnp.int32, (rows4, 1), 0) // tq
    sink_cols = []
    for kvh in range(N_KV):
        sc = jnp.zeros((rows4, 1), F32)
        for g_ in range(GROUP):
            sc = jnp.where(grp_col == g_, sinks_ref[kvh * GROUP + g_], sc)
        sink_cols.append(sc)
    key_col = lax.broadcasted_iota(jnp.int32, (rows4, nk), 1)

    for s in range(ns):
        for c in range(tm // tq):
            r0 = c * tq
            row0 = s * tm + r0
            for kvh in range(N_KV):
                q4 = jnp.concatenate(
                    [qn_ref[pl.ds(row0, tq), (kvh * GROUP + g_) * HEAD_DIM:(kvh * GROUP + g_ + 1) * HEAD_DIM]
                     for g_ in range(GROUP)], axis=0)
                ks = slice(kvh * HEAD_DIM, (kvh + 1) * HEAD_DIM)
                kb = kbuf_ref[s, pl.ds(r0, nk), ks]
                vb = vbuf_ref[s, pl.ds(r0, nk), ks]
                sc = lax.dot_general(q4, kb, (((1,), (1,)), ((), ())), preferred_element_type=F32)
                sc = sc + bias_ref[kvh]
                if first_pos is not None:
                    sc = jnp.where(key_col >= WINDOW - r0 - first_pos, sc, NEG)
                sink = sink_cols[kvh]
                mx = jnp.maximum(jnp.max(sc, axis=-1, keepdims=True), sink)
                p = jnp.exp(sc - mx)
                denom = jnp.sum(p, axis=-1, keepdims=True) + jnp.exp(sink - mx)
                o = _dot(p.astype(BF16), vb) / denom
                o_cat = jnp.concatenate([o[g_ * tq:(g_ + 1) * tq] for g_ in range(GROUP)], axis=1)
                ao_ref[pl.ds(row0, tq), kvh * GROUP * HEAD_DIM:(kvh + 1) * GROUP * HEAD_DIM] = o_cat.astype(BF16)
                yield
    if carry:
        kbuf_ref[:, :WINDOW, :] = kbuf_ref[:, tm:, :]
        vbuf_ref[:, :WINDOW, :] = vbuf_ref[:, tm:, :]


def _l1_mixer_count(ns, tm, tq):
    return 1 + ns * (tm // tq) * N_KV


def _l1_ffn_items(xres, ao_ref, w, act_ref, y_ref):
    (wo_ref, gf_ref, wg_ref, wu_ref, wd_ref) = w
    x1 = xres + _dot(ao_ref[...], wo_ref[...])
    yield
    yield from _ffn_items(x1, gf_ref, wg_ref, wu_ref, wd_ref, act_ref, y_ref)


def _l1_ffn_then_proj_items(x, x1_ref, ao_ref, w, act_ref, y_ref, *, do_ffn=True, do_proj=True):
    (wo_ref, gf_ref, wg_ref, wu_ref, wd_ref) = w
    if do_ffn:
        yield from _ffn_items(x1_ref[...], gf_ref, wg_ref, wu_ref, wd_ref, act_ref, y_ref)
    if do_proj:
        x1 = x + _dot(ao_ref[...], wo_ref[...])
        x1_ref[...] = x1
        yield


def _layer1_prompt_kernel(x_ref, sinks_ref, gm_ref, wqkv_ref, qg_ref, kg_ref, wo_ref, gf_ref,
                          wg_ref, wu_ref, wd_ref, y_ref, ko_ref, vo_ref,
                          qn_ref, kbuf_ref, vbuf_ref, ao_ref, act_ref, bias_ref, x1_ref, *, tm, n_t, n_tiles):
    i = pl.program_id(0)
    t = lax.rem(jnp.minimum(i, n_tiles - 1), n_t)

    @pl.when(i == 0)
    def _():
        _init_alibi_bias(bias_ref, CHUNK)

    @pl.when(t == 0)
    def _():
        kbuf_ref[:, :WINDOW, :] = jnp.zeros((1, WINDOW, KV_W), BF16)
        vbuf_ref[:, :WINDOW, :] = jnp.zeros((1, WINDOW, KV_W), BF16)

    def ffn(**stages):
        return _l1_ffn_then_proj_items(x_ref[0], x1_ref, ao_ref, (wo_ref, gf_ref, wg_ref, wu_ref, wd_ref),
                                       act_ref, y_ref.at[0], **stages)

    def mixer():
        return _l1_mixer_items(x_ref[0], (gm_ref, wqkv_ref, qg_ref, kg_ref), None, sinks_ref, ko_ref, vo_ref,
                               qn_ref, kbuf_ref, vbuf_ref, ao_ref, bias_ref,
                               ns=1, tm=tm, tq=CHUNK, first_pos=t * tm, carry=True)

    @pl.when(i == 0)
    def _():
        _run(mixer())
        _run(ffn(do_ffn=False))

    @pl.when((i > 0) & (i < n_tiles))
    def _():
        _interleave(ffn(), FFN_ITEMS + 1, mixer(), _l1_mixer_count(1, tm, CHUNK))

    @pl.when(i == n_tiles)
    def _():
        _run(ffn(do_proj=False))


def _layer1_sample_kernel(x_ref, kc_ref, vc_ref, sinks_ref, gm_ref, wqkv_ref, qg_ref, kg_ref, wo_ref, gf_ref,
                          wg_ref, wu_ref, wd_ref, y_ref, ko_ref, vo_ref,
                          qn_ref, kbuf_ref, vbuf_ref, ao_ref, act_ref, bias_ref, yflat_ref, *, ns, tm):
    m = ns * tm
    _init_alibi_bias(bias_ref, tm)
    kbuf_ref[:, :WINDOW, :] = kc_ref[...].astype(BF16)
    vbuf_ref[:, :WINDOW, :] = vc_ref[...].astype(BF16)
    x = x_ref[...].reshape(m, D_MODEL)
    _run(_l1_mixer_items(x, (gm_ref, wqkv_ref, qg_ref, kg_ref), (kc_ref, vc_ref), sinks_ref, ko_ref, vo_ref,
                         qn_ref, kbuf_ref, vbuf_ref, ao_ref, bias_ref,
                         ns=ns, tm=tm, tq=tm, first_pos=None, carry=False))
    _run(_l1_ffn_items(x, ao_ref, (wo_ref, gf_ref, wg_ref, wu_ref, wd_ref), act_ref, yflat_ref))
    y_ref[...] = yflat_ref[...].reshape(ns, tm, D_MODEL)


def _resident(arr, layer=None):
    if layer is None:
        nd = arr.ndim
        return pl.BlockSpec(arr.shape, lambda *_, _nd=nd: (0,) * _nd, pipeline_mode=pl.Buffered(1))
    nd = arr.ndim - 1
    return pl.BlockSpec((None,) + arr.shape[1:], lambda *_, _nd=nd, _l=layer: (_l,) + (0,) * _nd,
                        pipeline_mode=pl.Buffered(1))


def _params():
    return pltpu.CompilerParams(dimension_semantics=("arbitrary",), vmem_limit_bytes=VMEM_LIMIT)


def _tile_maps(n_t, n_tiles):
    def cur(i):
        j = jnp.minimum(i, n_tiles - 1)
        return (j // n_t, j % n_t, 0)

    def prev(i):
        j = jnp.maximum(i - 1, 0)
        return (j // n_t, j % n_t, 0)

    def cur_stream(i):
        return (jnp.minimum(i, n_tiles - 1) // n_t, 0, 0)

    return cur, prev, cur_stream


def _whole(shape):
    return pl.BlockSpec(shape, lambda i, _n=len(shape): (0,) * _n)


def _layer0_scratch(ns, tm):
    m = ns * tm
    return [pltpu.VMEM((ns, HALO + tm + BF16_ROWS, D_MODEL), F32),
            pltpu.VMEM((m, D_MODEL), F32),
            pltpu.VMEM((m, D_MODEL), BF16),
            pltpu.VMEM((m, D_FF), BF16)]


def _conv_tables(tm):
    rb = min(tm, CONV_ROWS)
    return [pltpu.VMEM((CONV_W, BF16_ROWS, D_MODEL), BF16),
            pltpu.VMEM((rb, SUBLANES * (rb + BF16_ROWS)), BF16)]


def _layer0(x, state, xs, states, w0, *, tm):
    nb, seq, _ = x.shape
    ns, tms, _ = xs.shape
    n_t = seq // tm
    n_tiles = nb * n_t
    tile = (1, tm, D_MODEL)
    last = n_tiles - 1
    tile_map = lambda i: (jnp.minimum(i, last) // n_t, jnp.minimum(i, last) % n_t, 0)
    stream_map = lambda i: (jnp.minimum(i, last) // n_t, 0, 0)
    once = lambda shape: pl.BlockSpec(shape, lambda i, _n=len(shape): (0,) * _n, pipeline_mode=pl.Buffered(1))
    return pl.pallas_call(
        functools.partial(_layer0_kernel, tm=tm, n_t=n_t, n_tiles=n_tiles),
        grid=(n_tiles + 1,),
        in_specs=[pl.BlockSpec(tile, tile_map), pl.BlockSpec((1, HALO, D_MODEL), stream_map),
                  once((ns, tms, D_MODEL)), once((ns, HALO, D_MODEL))] + [_resident(a, l) for a, l in w0],
        out_specs=[pl.BlockSpec(tile, tile_map), pl.BlockSpec((1, HALO, D_MODEL), stream_map),
                   _whole((ns, tms, D_MODEL)), _whole((ns, HALO, D_MODEL))],
        out_shape=[jax.ShapeDtypeStruct((nb, seq, D_MODEL), F32),
                   jax.ShapeDtypeStruct((nb, HALO, D_MODEL), F32),
                   jax.ShapeDtypeStruct((ns, tms, D_MODEL), F32),
                   jax.ShapeDtypeStruct((ns, HALO, D_MODEL), F32)],
        scratch_shapes=_layer0_scratch(1, tm) + _conv_tables(tm)
                       + [pltpu.VMEM((ns, HALO + tms + BF16_ROWS, D_MODEL), F32),
                          pltpu.VMEM((ns * tms, D_MODEL), F32), _conv_tables(tms)[1]],
        compiler_params=_params(),
        name="layer0",
    )(x, state, xs, states, *[a for a, _ in w0])


def _layer1_scratch(ns, tm, tq):
    m = ns * tm
    return [pltpu.VMEM((m, D_MODEL), BF16),
            pltpu.VMEM((ns, WINDOW + tm, KV_W), BF16),
            pltpu.VMEM((ns, WINDOW + tm, KV_W), BF16),
            pltpu.VMEM((m, D_MODEL), BF16),
            pltpu.VMEM((m, D_FF), BF16),
            pltpu.VMEM((N_KV, GROUP * tq, WINDOW + tq), F32),
            pltpu.VMEM((m, D_MODEL), F32)]


def _layer1_prompt(x, sinks, w1, *, tm):
    nb, seq, _ = x.shape
    n_t = seq // tm
    n_tiles = nb * n_t
    cur, prev, cur_stream = _tile_maps(n_t, n_tiles)
    tile = (1, tm, D_MODEL)
    cache = pl.BlockSpec((1, WINDOW, KV_W), cur_stream)
    return pl.pallas_call(
        functools.partial(_layer1_prompt_kernel, tm=tm, n_t=n_t, n_tiles=n_tiles),
        grid=(n_tiles + 1,),
        in_specs=[pl.BlockSpec(tile, cur), pl.BlockSpec(memory_space=pltpu.SMEM)]
                 + [_resident(a, l) for a, l in w1],
        out_specs=[pl.BlockSpec(tile, prev), cache, cache],
        out_shape=[jax.ShapeDtypeStruct((nb, seq, D_MODEL), F32),
                   jax.ShapeDtypeStruct((nb, WINDOW, KV_W), F32),
                   jax.ShapeDtypeStruct((nb, WINDOW, KV_W), F32)],
        scratch_shapes=_layer1_scratch(1, tm, CHUNK),
        compiler_params=_params(),
        name="layer1_prompt",
    )(x, sinks, *[a for a, _ in w1])


def _layer1_sample(x, kc, vc, sinks, w1):
    ns, tm, _ = x.shape
    cache = _whole((ns, WINDOW, KV_W))
    return pl.pallas_call(
        functools.partial(_layer1_sample_kernel, ns=ns, tm=tm),
        grid=(1,),
        in_specs=[_whole((ns, tm, D_MODEL)), cache, cache, pl.BlockSpec(memory_space=pltpu.SMEM)]
                 + [_resident(a, l) for a, l in w1],
        out_specs=[_whole((ns, tm, D_MODEL)), cache, cache],
        out_shape=[jax.ShapeDtypeStruct((ns, tm, D_MODEL), F32),
                   jax.ShapeDtypeStruct((ns, WINDOW, KV_W), F32),
                   jax.ShapeDtypeStruct((ns, WINDOW, KV_W), F32)],
        scratch_shapes=_layer1_scratch(ns, tm, tm),
        compiler_params=_params(),
        name="layer1_sample",
    )(x, kc, vc, sinks, *[a for a, _ in w1])


def kernel(x_prompt, x_sample, state_conv, cache_k, cache_v, g_mix, g_ffn, w_pw1, b_pw1, w_dw, b_dw,
           ln_g, ln_b, w_pw2, b_pw2, w_qkv, qn_g, kn_g, sinks, w_o, w_gate, w_up, w_down):
    nb = x_prompt.shape[0]
    ndb = x_sample.shape[0]
    row = lambda a: (a.reshape(1, -1), None)
    wg, wu, wd = w_gate.astype(BF16), w_up.astype(BF16), w_down.astype(BF16)
    w0 = [row(g_mix[0]), (w_pw1.astype(BF16), 0), row(b_pw1[0]), (w_dw, 0), row(b_dw[0]), row(ln_g[0]),
          row(ln_b[0]), (w_pw2.astype(BF16), 0), row(b_pw2[0]), row(g_ffn[0]), (wg, 0), (wu, 0), (wd, 0)]
    w1 = [row(g_mix[1]), (w_qkv.astype(BF16), 0), row(jnp.tile(qn_g[0], N_HEADS)), row(jnp.tile(kn_g[0], N_KV)),
          (w_o.astype(BF16), 0), row(g_ffn[1]), (wg, 1), (wu, 1), (wd, 1)]

    zero_state = jnp.zeros((nb, HALO, D_MODEL), F32)
    sample_state = jnp.pad(state_conv[0], ((0, 0), (HALO_PAD, 0), (0, 0)))
    xp1, cst_p, xs1, cst_s = _layer0(x_prompt, zero_state, x_sample, sample_state, w0, tm=PROMPT_TM)

    kc = cache_k[0].reshape(ndb, WINDOW, KV_W)
    vc = cache_v[0].reshape(ndb, WINDOW, KV_W)
    yp, kp, vp = _layer1_prompt(xp1, sinks[0], w1, tm=PROMPT_TM)
    ys, ks, vs = _layer1_sample(xs1, kc, vc, sinks[0], w1)

    kv_shape = lambda a: a.reshape(1, a.shape[0], WINDOW, N_KV, HEAD_DIM)
    return (yp, ys, cst_p[None, :, HALO_PAD:, :], cst_s[None, :, HALO_PAD:, :],
            kv_shape(kp), kv_shape(vp), kv_shape(ks), kv_shape(vs))
```
